```python
import math
import jax
import jax.numpy as jnp
from jax import lax
import numpy as np

D_MODEL = 2048
BATCH = 8
SEQ = 2048
DEPTH = 2

HEAD_DIM = 64
N_HEADS = D_MODEL // HEAD_DIM
N_HEADS_SB = N_HEADS // 4
N_HEADS_DIL = (N_HEADS - N_HEADS_SB) // 2
N_HEADS_FOX = N_HEADS - N_HEADS_SB - N_HEADS_DIL
MIX_WIDTH = N_HEADS * HEAD_DIM
W_SB = N_HEADS_SB * HEAD_DIM
W_FOX = N_HEADS_FOX * HEAD_DIM
W_DIL = N_HEADS_DIL * HEAD_DIM
IN_SIZES = (W_SB, W_SB, W_SB, W_FOX, W_FOX, W_FOX, W_DIL, W_DIL, W_DIL, N_HEADS_FOX)
VALUE_SLOTS = (2, 5, 8)
IN_WIDTH = sum(IN_SIZES)

QUERY_BLOCK = 128
DILATED_PATTERNS = ((128, 1), (512, 4), (2048, 16))
ROPE_THETA = 10000.0
FORGET_BIAS_INIT = 4.0

N_EXPERTS = 64
TOP_K = 6
EXPERT_HIDDEN = 512
SHARED_HIDDEN = 512
ROUTED_SCALE = 2.5
MOE_BLOCK = 128

DEEPNORM_ALPHA = (2 * DEPTH) ** 0.25
DEEPNORM_BETA = (8 * DEPTH) ** -0.25
LN_EPS = 1e-5
RMS_EPS = 1e-6

kernel_name = 'hybrid_stickbreak_fox_dilated_moe_deepnorm_adaln'


def layer_norm(x, g, b):
    xf = x.astype(jnp.float32)
    mu = jnp.mean(xf, axis=-1, keepdims=True)
    var = jnp.mean(jnp.square(xf - mu), axis=-1, keepdims=True)
    return ((xf - mu) * lax.rsqrt(var + LN_EPS) * g + b).astype(x.dtype)


def rms_norm(x, g):
    xf = x.astype(jnp.float32)
    return (xf * lax.rsqrt(jnp.mean(jnp.square(xf), axis=-1, keepdims=True) + RMS_EPS) * g).astype(x.dtype)


def split_columns(proj):
    parts, off = [], 0
    for n in IN_SIZES:
        parts.append(proj[..., off:off + n])
        off += n
    return parts


def to_heads(t):
    b, s, w = t.shape
    return t.reshape(b, s, w // HEAD_DIM, HEAD_DIM).transpose(0, 2, 1, 3)


def rope(t, positions):
    half = HEAD_DIM // 2
    inv_freq = ROPE_THETA ** (-jnp.arange(half, dtype=jnp.float32) / half)
    ang = positions.astype(jnp.float32)[:, None, :, None] * inv_freq
    cos, sin = jnp.cos(ang), jnp.sin(ang)
    tf = t.astype(jnp.float32)
    t1, t2 = tf[..., :half], tf[..., half:]
    return jnp.concatenate([t1 * cos - t2 * sin, t1 * sin + t2 * cos], axis=-1).astype(t.dtype)


def sweep_query_blocks(block_fn, seq_len):
    n_blocks = seq_len // QUERY_BLOCK
    out = lax.map(block_fn, jnp.arange(n_blocks))
    out = jnp.moveaxis(out, 0, 2)
    return out.reshape(out.shape[:2] + (seq_len, out.shape[-1]))


def stick_breaking_attention(q, k, v):
    seq_len = q.shape[2]
    key_idx = jnp.arange(seq_len)

    def block(b):
        start = b * QUERY_BLOCK
        qb = lax.dynamic_slice_in_dim(q, start, QUERY_BLOCK, axis=2)
        z = jnp.einsum('bhqd,bhkd->bhqk', qb, k, preferred_element_type=jnp.float32) * HEAD_DIM ** -0.5
        q_idx = start + jnp.arange(QUERY_BLOCK)
        past = key_idx[None, :] < q_idx[:, None]
        log_keep = jnp.where(past, jax.nn.log_sigmoid(-z), 0.0)
        later = lax.cumsum(log_keep, axis=3, reverse=True) - log_keep
        w = jnp.where(past, jnp.exp(jax.nn.log_sigmoid(z) + later), 0.0)
        return jnp.einsum('bhqk,bhkd->bhqd', w.astype(v.dtype), v)

    return sweep_query_blocks(block, seq_len)


def forgetting_attention(q, k, v, log_f):
    seq_len = q.shape[2]
    key_idx = jnp.arange(seq_len)
    cum = jnp.cumsum(log_f, axis=-1)

    def block(b):
        start = b * QUERY_BLOCK
        qb = lax.dynamic_slice_in_dim(q, start, QUERY_BLOCK, axis=2)
        cum_q = lax.dynamic_slice_in_dim(cum, start, QUERY_BLOCK, axis=2)
        z = jnp.einsum('bhqd,bhkd->bhqk', qb, k, preferred_element_type=jnp.float32) * HEAD_DIM ** -0.5
        logits = z + (cum_q[..., :, None] - cum[..., None, :])
        q_idx = start + jnp.arange(QUERY_BLOCK)
        causal = key_idx[None, :] <= q_idx[:, None]
        p = jax.nn.softmax(jnp.where(causal, logits, -jnp.inf), axis=-1)
        return jnp.einsum('bhqk,bhkd->bhqd', p.astype(v.dtype), v)

    return sweep_query_blocks(block, seq_len)


def sliding_window_attention(q, k, v, window):
    length = q.shape[-2]
    blk = math.gcd(length, QUERY_BLOCK)
    n_blocks = length // blk
    span = window + blk
    pad = [(0, 0)] * (q.ndim - 2) + [(window, 0), (0, 0)]
    kp, vp = jnp.pad(k, pad), jnp.pad(v, pad)
    starts = jnp.arange(n_blocks) * blk
    slab = starts[:, None] + jnp.arange(span)[None, :]
    ks = jnp.take(kp, slab, axis=-2)
    vs = jnp.take(vp, slab, axis=-2)
    qb = q.reshape(q.shape[:-2] + (n_blocks, blk, q.shape[-1]))
    logits = jnp.einsum('...nqd,...nkd->...nqk', qb, ks, preferred_element_type=jnp.float32)
    dist = jnp.arange(blk)[:, None] + window - jnp.arange(span)[None, :]
    key_pos = starts[:, None] - window + jnp.arange(span)[None, :]
    valid = ((dist >= 0) & (dist <= window))[None] & (key_pos >= 0)[:, None, :]
    logits = jnp.where(valid, logits, -jnp.inf)
    m = jnp.max(logits, axis=-1, keepdims=True)
    p = jnp.exp(logits - m)
    denom = jnp.sum(p, axis=-1)
    o = jnp.einsum('...nqk,...nkd->...nqd', p.astype(v.dtype), vs, preferred_element_type=jnp.float32)
    o = o / denom[..., None]
    lse = m[..., 0] + jnp.log(denom)
    return o.reshape(q.shape).astype(v.dtype), lse.reshape(q.shape[:-1])


def to_residues(t, dil):
    b, h, s = t.shape[:3]
    return jnp.swapaxes(t.reshape((b, h, s // dil, dil) + t.shape[3:]), 2, 3)


def from_residues(t):
    t = jnp.swapaxes(t, 2, 3)
    return t.reshape(t.shape[:2] + (t.shape[2] * t.shape[3],) + t.shape[4:])


def dilated_attention(q, k, v):
    q = q * HEAD_DIM ** -0.5
    outs, lses = [], []
    for window, dil in DILATED_PATTERNS:
        o, lse = sliding_window_attention(to_residues(q, dil), to_residues(k, dil),
                                          to_residues(v, dil), window // dil)
        outs.append(from_residues(o))
        lses.append(from_residues(lse))
    wts = jax.nn.softmax(jnp.stack(lses), axis=0)
    o = jnp.sum(wts[..., None] * jnp.stack(outs).astype(jnp.float32), axis=0)
    return o.astype(v.dtype)


def token_mixer(h, positions, w_in, b_forget, head_norm_g, w_out):
    b, s, _ = h.shape
    proj = h @ w_in
    (q_sb, k_sb, v_sb, q_fx, k_fx, v_fx, q_dl, k_dl, v_dl, f_logit) = split_columns(proj)
    o_sb = stick_breaking_attention(to_heads(q_sb), to_heads(k_sb), to_heads(v_sb))
    log_f = jax.nn.log_sigmoid((f_logit + b_forget).astype(jnp.float32)).transpose(0, 2, 1)
    o_fx = forgetting_attention(to_heads(q_fx), to_heads(k_fx), to_heads(v_fx), log_f)
    o_dl = dilated_attention(rope(to_heads(q_dl), positions), rope(to_heads(k_dl), positions),
                             to_heads(v_dl))
    o = jnp.concatenate([o_sb, o_fx, o_dl], axis=1)
    o = rms_norm(o, head_norm_g.reshape(N_HEADS, 1, HEAD_DIM))
    o = o.transpose(0, 2, 1, 3).reshape(b, s, MIX_WIDTH)
    return o @ w_out


def swiglu(x, w_gate, w_up, w_down):
    return (jax.nn.silu(x @ w_gate) * (x @ w_up)) @ w_down


def grouped_experts(hf, expert_idx, gates, w_gate, w_up, w_down):
    n_tok = hf.shape[0]
    n_assign = n_tok * TOP_K
    flat_e = expert_idx.reshape(-1)
    flat_tok = jnp.arange(n_assign, dtype=jnp.int32) // TOP_K
    flat_w = gates.reshape(-1)
    order = jnp.argsort(flat_e, stable=True)
    e_sorted = flat_e[order]
    counts = jnp.bincount(flat_e, length=N_EXPERTS)
    padded = (counts + MOE_BLOCK - 1) // MOE_BLOCK * MOE_BLOCK
    start = jnp.cumsum(counts) - counts
    pad_end = jnp.cumsum(padded)
    pad_start = pad_end - padded
    dest = pad_start[e_sorted] + jnp.arange(n_assign, dtype=jnp.int32) - start[e_sorted]
    n_blocks = -(-(n_assign + N_EXPERTS * (MOE_BLOCK - 1)) // MOE_BLOCK)
    rows = n_blocks * MOE_BLOCK
    row_tok = jnp.zeros((rows,), jnp.int32).at[dest].set(flat_tok[order])
    row_w = jnp.zeros((rows,), flat_w.dtype).at[dest].set(flat_w[order])
    block_e = jnp.searchsorted(pad_end, jnp.arange(n_blocks) * MOE_BLOCK, side='right')
    block_e = jnp.minimum(block_e, N_EXPERTS - 1)

    def step(acc, blk):
        tok, wt, e = blk
        y = swiglu(hf[tok], w_gate[e], w_up[e], w_down[e]) * wt[:, None]
        return acc.at[tok].add(y), None

    acc, _ = lax.scan(step, jnp.zeros_like(hf),
                      (row_tok.reshape(n_blocks, MOE_BLOCK), row_w.reshape(n_blocks, MOE_BLOCK), block_e))
    return acc


def moe_ffn(h, w_router, router_bias, w_gate, w_up, w_down, ws_gate, ws_up, ws_down):
    b, s, d = h.shape
    hf = h.reshape(b * s, d)
    scores = jax.nn.sigmoid(hf.astype(jnp.float32) @ w_router.astype(jnp.float32))
    _, idx = lax.top_k(scores + router_bias.astype(jnp.float32), TOP_K)
    sel = jnp.take_along_axis(scores, idx, axis=-1)
    gates = (sel / jnp.sum(sel, axis=-1, keepdims=True) * ROUTED_SCALE).astype(h.dtype)
    routed = grouped_experts(hf, idx, gates, w_gate, w_up, w_down)
    shared = swiglu(hf, ws_gate, ws_up, ws_down)
    return (routed + shared).reshape(b, s, d)


def setup_inputs(seed: int = 0) -> dict:
    key = jax.random.key(seed)
    ks = jax.random.split(key, 22)
    d, L, e, fh, sh = D_MODEL, DEPTH, N_EXPERTS, EXPERT_HIDDEN, SHARED_HIDDEN

    def dense(k, shape, fan_in, gain=1.0):
        return jax.random.normal(k, shape, jnp.float32) * (gain * fan_in ** -0.5)

    def near_one(k, shape):
        return 1.0 + 0.02 * jax.random.normal(k, shape, jnp.float32)

    def small(k, shape, scale=0.02):
        return scale * jax.random.normal(k, shape, jnp.float32)

    col_gain = jnp.concatenate([jnp.full((n,), DEEPNORM_BETA if i in VALUE_SLOTS else 1.0, jnp.float32)
                                for i, n in enumerate(IN_SIZES)])
    offsets = jax.random.randint(ks[2], (BATCH, 1), 0, SEQ, dtype=jnp.int32)
    return {
        'x': jax.random.normal(ks[0], (BATCH, SEQ, d), jnp.float32),
        'c': jax.random.normal(ks[1], (BATCH, d), jnp.float32),
        'positions': offsets + jnp.arange(SEQ, dtype=jnp.int32)[None, :],
        'w_ada': dense(ks[3], (L, d, 6 * d), d),
        'b_ada': small(ks[4], (L, 6 * d)),
        'w_in': dense(ks[5], (L, d, IN_WIDTH), d) * col_gain,
        'b_forget': FORGET_BIAS_INIT + small(ks[6], (L, N_HEADS_FOX), 0.1),
        'head_norm_g': near_one(ks[7], (L, MIX_WIDTH)),
        'w_out': dense(ks[8], (L, MIX_WIDTH, d), MIX_WIDTH, DEEPNORM_BETA),
        'ln1_g': near_one(ks[9], (L, d)),
        'ln1_b': small(ks[10], (L, d)),
        'w_router': dense(ks[11], (L, d, e), d),
        'router_bias': small(ks[12], (L, e), 0.01),
        'w_exp_gate': dense(ks[13], (L, e, d, fh), d, DEEPNORM_BETA),
        'w_exp_up': dense(ks[14], (L, e, d, fh), d, DEEPNORM_BETA),
        'w_exp_down': dense(ks[15], (L, e, fh, d), fh, DEEPNORM_BETA),
        'w_sh_gate': dense(ks[16], (L, d, sh), d, DEEPNORM_BETA),
        'w_sh_up': dense(ks[17], (L, d, sh), d, DEEPNORM_BETA),
        'w_sh_down': dense(ks[18], (L, sh, d), sh, DEEPNORM_BETA),
        'ln2_g': near_one(ks[19], (L, d)),
        'ln2_b': small(ks[20], (L, d)),
    }


def reference(x, c, positions, w_ada, b_ada, w_in, b_forget, head_norm_g, w_out, ln1_g, ln1_b,
              w_router, router_bias, w_exp_gate, w_exp_up, w_exp_down, w_sh_gate, w_sh_up, w_sh_down,
              ln2_g, ln2_b):
    c_act = jax.nn.silu(c)
    for l in range(DEPTH):
        mod = (c_act @ w_ada[l] + b_ada[l])[:, None, :]
        shift1, scale1, gate1, shift2, scale2, gate2 = jnp.split(mod, 6, axis=-1)
        h = x * (1.0 + scale1) + shift1
        mix = token_mixer(h, positions, w_in[l], b_forget[l], head_norm_g[l], w_out[l])
        x = layer_norm(DEEPNORM_ALPHA * x + gate1 * mix, ln1_g[l], ln1_b[l])
        h = x * (1.0 + scale2) + shift2
        ffn = moe_ffn(h, w_router[l], router_bias[l], w_exp_gate[l], w_exp_up[l], w_exp_down[l],
                      w_sh_gate[l], w_sh_up[l], w_sh_down[l])
        x = layer_norm(DEEPNORM_ALPHA * x + gate2 * ffn, ln2_g[l], ln2_b[l])
    return x
```

```python
import functools

import jax
import jax.numpy as jnp
from jax import lax
from jax.experimental import pallas as pl
from jax.experimental.pallas import tpu as pltpu

F32 = jnp.float32
BF16 = jnp.bfloat16
I32 = jnp.int32

D_MODEL = 2048
DEPTH = 2
HEAD_DIM = 64
N_HEADS_SB = 8
N_HEADS_FOX = 12
N_HEADS_DIL = 12
W_SB = N_HEADS_SB * HEAD_DIM
W_FOX = N_HEADS_FOX * HEAD_DIM
W_DIL = N_HEADS_DIL * HEAD_DIM
DILATIONS = (1, 4, 16)
DIL_SPAN = 128
ROPE_THETA = 10000.0
N_EXPERTS = 64
TOP_K = 6
ROUTED_SCALE = 2.5
DEEPNORM_ALPHA = (2 * DEPTH) ** 0.25
LN_EPS = 1e-5
RMS_EPS = 1e-6
QK_SCALE = HEAD_DIM ** -0.5
NEG_INF = float("-inf")

LANES = 128
ATTN_BLOCK = 256
MOE_ROWS = 256
COMBINE_TOKENS = 128
VMEM_LIMIT = 56 * 1024 * 1024


def _params(semantics, vmem=VMEM_LIMIT):
    return pltpu.CompilerParams(dimension_semantics=semantics, vmem_limit_bytes=vmem)


def _dot(a, b):
    return jnp.dot(a, b, preferred_element_type=F32)


def _dot_nt(a, b):
    return lax.dot_general(a, b, (((1,), (1,)), ((), ())), preferred_element_type=F32)


def _split2(a):
    hi = a.astype(BF16)
    lo = (a - hi.astype(F32)).astype(BF16)
    return hi, lo


def _split3(a):
    hi = a.astype(BF16)
    r1 = a - hi.astype(F32)
    mid = r1.astype(BF16)
    lo = (r1 - mid.astype(F32)).astype(BF16)
    return hi, mid, lo


def _dot_f32(a, b):
    ah, al = _split2(a)
    bh, bl = _split2(b)
    return _dot(ah, bh) + _dot(ah, bl) + _dot(al, bh)


def _softplus(z):
    return jnp.maximum(z, 0.0) + jnp.log1p(jnp.exp(-jnp.abs(z)))


def _silu(z):
    return z * jax.nn.sigmoid(z)


def _layer_norm(r, g, b):
    mu = jnp.mean(r, axis=-1, keepdims=True)
    d = r - mu
    var = jnp.mean(d * d, axis=-1, keepdims=True)
    return d * lax.rsqrt(var + LN_EPS) * g + b


def _pair_rms_norm(o, g, lane):
    left = lane < HEAD_DIM
    sq = o * o
    s0 = jnp.sum(jnp.where(left, sq, 0.0), axis=-1, keepdims=True)
    s1 = jnp.sum(jnp.where(left, 0.0, sq), axis=-1, keepdims=True)
    ms = jnp.where(left, s0, s1) * (1.0 / HEAD_DIM)
    return o * lax.rsqrt(ms + RMS_EPS) * g


def _ada_kernel(c_ref, w_ref, b_ref, o_ref):
    o_ref[0] = _dot_f32(_silu(c_ref[...]), w_ref[0]) + b_ref[0]


def _ada_modulation(c, w_ada, b_ada, tn=1024):
    depth, d, n = w_ada.shape
    bsz = c.shape[0]
    return pl.pallas_call(
        _ada_kernel,
        grid=(depth, n // tn),
        in_specs=[
            pl.BlockSpec((bsz, d), lambda l, j: (0, 0)),
            pl.BlockSpec((1, d, tn), lambda l, j: (l, 0, j)),
            pl.BlockSpec((1, 1, tn), lambda l, j: (l, 0, j)),
        ],
        out_specs=pl.BlockSpec((1, bsz, tn), lambda l, j: (l, 0, j)),
        out_shape=jax.ShapeDtypeStruct((depth, bsz, n), F32),
        compiler_params=_params(("arbitrary", "arbitrary")),
        name="ada_modulation",
    )(c, w_ada, b_ada.reshape(depth, 1, n))


def _rope_kernel(pos_ref, invf_ref, sign_ref, cos_ref, sin_ref):
    ang = pos_ref[0].astype(F32) * invf_ref[...]
    cos_ref[0] = jnp.cos(ang)
    sin_ref[0] = jnp.sin(ang) * sign_ref[...]


def _rope_tables(positions, ts=512):
    bsz, seq = positions.shape
    half = HEAD_DIM // 2
    inv_freq = ROPE_THETA ** (-jnp.arange(half, dtype=F32) / half)
    inv_lane = jnp.tile(inv_freq, LANES // half).reshape(1, LANES)
    sign = jnp.tile(jnp.concatenate([-jnp.ones((half,), F32), jnp.ones((half,), F32)]),
                    LANES // HEAD_DIM).reshape(1, LANES)
    tab = jax.ShapeDtypeStruct((bsz, seq, LANES), F32)
    return pl.pallas_call(
        _rope_kernel,
        grid=(bsz, seq // ts),
        in_specs=[
            pl.BlockSpec((1, ts, 1), lambda b, i: (b, i, 0)),
            pl.BlockSpec((1, LANES), lambda b, i: (0, 0)),
            pl.BlockSpec((1, LANES), lambda b, i: (0, 0)),
        ],
        out_specs=[pl.BlockSpec((1, ts, LANES), lambda b, i: (b, i, 0))] * 2,
        out_shape=[tab, tab],
        compiler_params=_params(("arbitrary", "arbitrary")),
        name="rope_tables",
    )(positions.reshape(bsz, seq, 1), inv_lane, sign)


def _modmm_kernel(x_ref, sc_ref, sh_ref, w_ref, o_ref):
    h = x_ref[0] * (1.0 + sc_ref[0]) + sh_ref[0]
    o_ref[0] = _dot(h.astype(BF16), w_ref[...]).astype(o_ref.dtype)


def _modulated_matmul(x, scale, shift, w, out_dtype, tm, tn):
    bsz, seq, d = x.shape
    n = w.shape[1]
    return pl.pallas_call(
        _modmm_kernel,
        grid=(n // tn, bsz, seq // tm),
        in_specs=[
            pl.BlockSpec((1, tm, d), lambda j, b, i: (b, i, 0)),
            pl.BlockSpec((1, 1, d), lambda j, b, i: (b, 0, 0)),
            pl.BlockSpec((1, 1, d), lambda j, b, i: (b, 0, 0)),
            pl.BlockSpec((d, tn), lambda j, b, i: (0, j)),
        ],
        out_specs=pl.BlockSpec((1, tm, tn), lambda j, b, i: (b, i, j)),
        out_shape=jax.ShapeDtypeStruct((bsz, seq, n), out_dtype),
        compiler_params=_params(("arbitrary", "arbitrary", "arbitrary")),
        name="modulated_projection",
    )(x, scale, shift, w)


def _sb_kernel(q_ref, k_ref, v_ref, g_ref, o_ref, *, blk):
    qi = pl.program_id(2)
    lane = lax.broadcasted_iota(I32, (1, LANES), 1)
    row = lax.broadcasted_iota(I32, (blk, blk), 0)
    col = lax.broadcasted_iota(I32, (blk, blk), 1)
    past = col < row
    later_keys = jnp.where(row > col, 1.0, 0.0).astype(BF16)
    q = q_ref[0]
    heads = []
    for h in range(2):
        in_head = (lane < HEAD_DIM) if h == 0 else (lane >= HEAD_DIM)
        qh = jnp.where(in_head, q, jnp.zeros_like(q))

        def block(kj, carry, masked, qh=qh):
            o, tail = carry
            start = pl.multiple_of(kj * blk, blk)
            kb = k_ref[0, pl.ds(start, blk), :]
            vb = v_ref[0, pl.ds(start, blk), :]
            z = _dot_nt(qh, kb) * QK_SCALE
            sp = _softplus(z)
            log_keep = -sp
            if masked:
                log_keep = jnp.where(past, log_keep, 0.0)
            hi, lo = _split2(log_keep)
            later = _dot(hi, later_keys) + _dot(lo, later_keys) + tail
            w = jnp.exp(z - sp + later)
            if masked:
                w = jnp.where(past, w, 0.0)
            o = o + _dot(w.astype(BF16), vb)
            tail = tail + jnp.sum(log_keep, axis=-1, keepdims=True)
            return o, tail

        carry = (jnp.zeros((blk, LANES), F32), jnp.zeros((blk, 1), F32))
        carry = block(qi, carry, True)
        carry = lax.fori_loop(0, qi, lambda it, c, block=block: block(qi - 1 - it, c, False), carry)
        heads.append(carry[0])
    o = jnp.where(lane < HEAD_DIM, heads[0], heads[1])
    o_ref[0] = _pair_rms_norm(o, g_ref[...], lane).astype(o_ref.dtype)


def _sb_attention(proj, gain, blk=ATTN_BLOCK):
    bsz, seq, _ = proj.shape
    pairs = W_SB // LANES
    return pl.pallas_call(
        functools.partial(_sb_kernel, blk=blk),
        grid=(bsz, pairs, seq // blk),
        in_specs=[
            pl.BlockSpec((1, blk, LANES), lambda b, p, i: (b, i, p)),
            pl.BlockSpec((1, seq, LANES), lambda b, p, i: (b, 0, pairs + p)),
            pl.BlockSpec((1, seq, LANES), lambda b, p, i: (b, 0, 2 * pairs + p)),
            pl.BlockSpec((1, LANES), lambda b, p, i: (0, p)),
        ],
        out_specs=pl.BlockSpec((1, blk, LANES), lambda b, p, i: (b, i, p)),
        out_shape=jax.ShapeDtypeStruct((bsz, seq, W_SB), BF16),
        compiler_params=_params(("arbitrary", "arbitrary", "arbitrary")),
        name="stick_breaking_attention",
    )(proj, proj, proj, gain.reshape(1, W_SB))


def _fox_prep_kernel(f_ref, b_ref, col_ref, row_ref, *, blk):
    seq = f_ref.shape[1]
    x = f_ref[0] + b_ref[...]
    log_f = -_softplus(-x)
    r = lax.broadcasted_iota(I32, (blk, blk), 0)
    c = lax.broadcasted_iota(I32, (blk, blk), 1)
    upto = jnp.where(c <= r, 1.0, 0.0).astype(BF16)
    carry = jnp.zeros((1, LANES), F32)
    parts = []
    for i in range(seq // blk):
        hi, mid, lo = _split3(log_f[i * blk:(i + 1) * blk])
        cs = _dot(upto, hi) + _dot(upto, mid) + _dot(upto, lo) + carry
        parts.append(cs)
        carry = cs[blk - 1:blk, :]
    cum = jnp.concatenate(parts, axis=0)
    col_ref[0] = cum
    row_ref[0] = cum.T


def _fox_prefix(f_logits, f_block, b_forget):
    bsz, seq, _ = f_logits.shape
    bias = jnp.zeros((1, LANES), F32).at[0, :N_HEADS_FOX].set(b_forget)
    return pl.pallas_call(
        functools.partial(_fox_prep_kernel, blk=256),
        grid=(bsz,),
        in_specs=[
            pl.BlockSpec((1, seq, LANES), lambda b: (b, 0, f_block)),
            pl.BlockSpec((1, LANES), lambda b: (0, 0)),
        ],
        out_specs=[
            pl.BlockSpec((1, seq, LANES), lambda b: (b, 0, 0)),
            pl.BlockSpec((1, LANES, seq), lambda b: (b, 0, 0)),
        ],
        out_shape=[jax.ShapeDtypeStruct((bsz, seq, LANES), F32),
                   jax.ShapeDtypeStruct((bsz, LANES, seq), F32)],
        compiler_params=_params(("arbitrary",)),
        name="forget_prefix",
    )(f_logits, bias)


def _fox_kernel(q_ref, k_ref, v_ref, cq_ref, ck_ref, g_ref, o_ref, *, blk):
    pair = pl.program_id(1)
    qi = pl.program_id(2)
    lane = lax.broadcasted_iota(I32, (1, LANES), 1)
    row = lax.broadcasted_iota(I32, (blk, blk), 0)
    col = lax.broadcasted_iota(I32, (blk, blk), 1)
    causal = col <= row
    q = q_ref[0]
    cum_q_all = cq_ref[0]
    heads = []
    for h in range(2):
        in_head = (lane < HEAD_DIM) if h == 0 else (lane >= HEAD_DIM)
        qh = jnp.where(in_head, q, jnp.zeros_like(q))
        cum_q = jnp.sum(jnp.where(lane == 2 * pair + h, cum_q_all, 0.0), axis=-1, keepdims=True)

        def block(kj, carry, masked, qh=qh, cum_q=cum_q, h=h):
            m, l, o = carry
            start = pl.multiple_of(kj * blk, blk)
            kb = k_ref[0, pl.ds(start, blk), :]
            vb = v_ref[0, pl.ds(start, blk), :]
            cum_k = ck_ref[0, 0, h:h + 1, pl.ds(start, blk)]
            z = _dot_nt(qh, kb) * QK_SCALE + (cum_q - cum_k)
            if masked:
                z = jnp.where(causal, z, NEG_INF)
            m_new = jnp.maximum(m, jnp.max(z, axis=-1, keepdims=True))
            p = jnp.exp(z - m_new)
            a = jnp.exp(m - m_new)
            l = a * l + jnp.sum(p, axis=-1, keepdims=True)
            o = a * o + _dot(p.astype(BF16), vb)
            return m_new, l, o

        carry = (jnp.full((blk, 1), NEG_INF, F32), jnp.zeros((blk, 1), F32),
                 jnp.zeros((blk, LANES), F32))
        carry = block(qi, carry, True)
        carry = lax.fori_loop(0, qi, lambda kj, c, block=block: block(kj, c, False), carry)
        heads.append(carry[2] / carry[1])
    o = jnp.where(lane < HEAD_DIM, heads[0], heads[1])
    o_ref[0] = _pair_rms_norm(o, g_ref[...], lane).astype(o_ref.dtype)


def _fox_attention(proj, col_base, cum_col, cum_row, gain, blk=ATTN_BLOCK):
    bsz, seq, _ = proj.shape
    pairs = W_FOX // LANES
    return pl.pallas_call(
        functools.partial(_fox_kernel, blk=blk),
        grid=(bsz, pairs, seq // blk),
        in_specs=[
            pl.BlockSpec((1, blk, LANES), lambda b, p, i: (b, i, col_base + p)),
            pl.BlockSpec((1, seq, LANES), lambda b, p, i: (b, 0, col_base + pairs + p)),
            pl.BlockSpec((1, seq, LANES), lambda b, p, i: (b, 0, col_base + 2 * pairs + p)),
            pl.BlockSpec((1, blk, LANES), lambda b, p, i: (b, i, 0)),
            pl.BlockSpec((1, 1, 2, seq), lambda b, p, i: (b, p, 0, 0)),
            pl.BlockSpec((1, LANES), lambda b, p, i: (0, p)),
        ],
        out_specs=pl.BlockSpec((1, blk, LANES), lambda b, p, i: (b, i, p)),
        out_shape=jax.ShapeDtypeStruct((bsz, seq, W_FOX), BF16),
        compiler_params=_params(("arbitrary", "arbitrary", "arbitrary")),
        name="forgetting_attention",
    )(proj, proj, proj, cum_col, cum_row, gain.reshape(1, W_FOX))


def _dil_kernel(q_ref, k_ref, v_ref, cos_ref, sin_ref, g_ref, o_ref,
                qs, ks, vs, og, lg, *, seq, pad):
    lane = lax.broadcasted_iota(I32, (1, LANES), 1)
    first_half = (lane & (HEAD_DIM - 1)) < HEAD_DIM // 2
    cos = cos_ref[0]
    sin = sin_ref[0]

    def rope(t):
        partner = jnp.where(first_half, pltpu.roll(t, LANES - HEAD_DIM // 2, 1),
                            pltpu.roll(t, HEAD_DIM // 2, 1))
        return t * cos + partner * sin

    qs[...] = rope(q_ref[0]) * QK_SCALE
    ks[0:pad, :] = jnp.zeros((pad, LANES), F32)
    vs[0:pad, :] = jnp.zeros((pad, LANES), F32)
    ks[pad:pad + seq, :] = rope(k_ref[0])
    vs[pad:pad + seq, :] = v_ref[0]

    span = DIL_SPAN
    qpos = lax.broadcasted_iota(I32, (span, 2 * span), 0)
    kpos = lax.broadcasted_iota(I32, (span, 2 * span), 1)
    band = (kpos >= qpos) & (kpos <= qpos + span)

    for gi, dil in enumerate(DILATIONS):
        nblk = seq // dil // span

        def body(idx, _, gi=gi, dil=dil):
            r = lax.rem(idx, dil)
            n = lax.div(idx, dil)
            q0 = r + n * (span * dil)
            k0 = pad + q0 - span * dil
            if dil == 1:
                q_rows = pl.ds(q0, span)
                k_rows = pl.ds(k0, 2 * span)
            else:
                q_rows = pl.ds(q0, span, stride=dil)
                k_rows = pl.ds(k0, 2 * span, stride=dil)
            qb = qs[q_rows, :]
            kb = ks[k_rows, :].astype(BF16)
            vb = vs[k_rows, :].astype(BF16)
            valid = band & (kpos + (n - 1) * span >= 0)
            outs, lses = [], []
            for h in range(2):
                in_head = (lane < HEAD_DIM) if h == 0 else (lane >= HEAD_DIM)
                qh = jnp.where(in_head, qb, 0.0).astype(BF16)
                z = jnp.where(valid, _dot_nt(qh, kb), NEG_INF)
                m = jnp.max(z, axis=-1, keepdims=True)
                p = jnp.exp(z - m)
                den = jnp.sum(p, axis=-1, keepdims=True)
                outs.append(_dot(p.astype(BF16), vb) / den)
                lses.append(m + jnp.log(den))
            og[gi, q_rows, :] = jnp.where(lane < HEAD_DIM, outs[0], outs[1])
            lg[gi, q_rows, :] = jnp.where(lane < HEAD_DIM, lses[0], lses[1])
            return 0

        lax.fori_loop(0, dil * nblk, body, 0)

    lse = lg[...]
    top = jnp.max(lse, axis=0)
    wts = jnp.exp(lse - top[None])
    o = jnp.sum(wts * og[...], axis=0) / jnp.sum(wts, axis=0)
    o_ref[0] = _pair_rms_norm(o, g_ref[...], lane).astype(o_ref.dtype)


def _dil_attention(proj, cos_t, sin_t, gain):
    bsz, seq, _ = proj.shape
    pairs = W_DIL // LANES
    pad = DIL_SPAN * max(DILATIONS)
    assert all(seq % (d * DIL_SPAN) == 0 for d in DILATIONS)
    return pl.pallas_call(
        functools.partial(_dil_kernel, seq=seq, pad=pad),
        grid=(bsz, pairs),
        in_specs=[
            pl.BlockSpec((1, seq, LANES), lambda b, p: (b, 0, p)),
            pl.BlockSpec((1, seq, LANES), lambda b, p: (b, 0, pairs + p)),
            pl.BlockSpec((1, seq, LANES), lambda b, p: (b, 0, 2 * pairs + p)),
            pl.BlockSpec((1, seq, LANES), lambda b, p: (b, 0, 0)),
            pl.BlockSpec((1, seq, LANES), lambda b, p: (b, 0, 0)),
            pl.BlockSpec((1, LANES), lambda b, p: (0, p)),
        ],
        out_specs=pl.BlockSpec((1, seq, LANES), lambda b, p: (b, 0, p)),
        out_shape=jax.ShapeDtypeStruct((bsz, seq, W_DIL), BF16),
        scratch_shapes=[
            pltpu.VMEM((seq, LANES), F32),
            pltpu.VMEM((pad + seq, LANES), F32),
            pltpu.VMEM((pad + seq, LANES), F32),
            pltpu.VMEM((len(DILATIONS), seq, LANES), F32),
            pltpu.VMEM((len(DILATIONS), seq, LANES), F32),
        ],
        compiler_params=_params(("arbitrary", "arbitrary")),
        name="dilated_attention",
    )(proj, proj, proj, cos_t, sin_t, gain.reshape(1, W_DIL))


def _oproj_kernel(osb_ref, ofx_ref, odl_ref, w1_ref, w2_ref, w3_ref, x_ref, gate_ref, sc_ref, sh_ref,
                  lng_ref, lnb_ref, wr_ref, rb_ref, x1_ref, h2_ref, idx_ref, gw_ref):
    mix = (_dot(osb_ref[0], w1_ref[...]) + _dot(ofx_ref[0], w2_ref[...])
           + _dot(odl_ref[0], w3_ref[...]))
    x1 = _layer_norm(DEEPNORM_ALPHA * x_ref[0] + gate_ref[0] * mix, lng_ref[...], lnb_ref[...])
    x1_ref[0] = x1
    h2 = x1 * (1.0 + sc_ref[0]) + sh_ref[0]
    h2_ref[0] = h2

    scores = jax.nn.sigmoid(_dot_f32(h2, wr_ref[...]))
    tm = scores.shape[0]
    lane = lax.broadcasted_iota(I32, (1, LANES), 1)
    lane_f = lane.astype(F32)
    sel = jnp.where(lane < N_EXPERTS, scores + rb_ref[...], NEG_INF)
    idx_acc = jnp.zeros((tm, LANES), I32)
    gate_acc = jnp.zeros((tm, LANES), F32)
    total = jnp.zeros((tm, 1), F32)
    for k in range(TOP_K):
        best = jnp.max(sel, axis=-1, keepdims=True)
        pick = jnp.min(jnp.where(sel == best, lane_f, float(LANES)), axis=-1, keepdims=True)
        chosen = lane_f == pick
        s = jnp.sum(jnp.where(chosen, scores, 0.0), axis=-1, keepdims=True)
        sel = jnp.where(chosen, NEG_INF, sel)
        idx_acc = jnp.where(lane == k, pick.astype(I32), idx_acc)
        gate_acc = jnp.where(lane == k, s, gate_acc)
        total = total + s
    idx_ref[0] = idx_acc[:, :8]
    gw_ref[0] = (gate_acc / total * ROUTED_SCALE)[:, :8]


def _output_projection(o_sb, o_fx, o_dl, w_out, x, gate1, scale2, shift2, ln_g, ln_b, w_router,
                       router_bias, tm=256):
    bsz, seq, d = x.shape
    w1 = w_out[:W_SB].astype(BF16)
    w2 = w_out[W_SB:W_SB + W_FOX].astype(BF16)
    w3 = w_out[W_SB + W_FOX:].astype(BF16)
    wr = jnp.zeros((d, LANES), F32).at[:, :N_EXPERTS].set(w_router.astype(F32))
    rb = jnp.zeros((1, LANES), F32).at[0, :N_EXPERTS].set(router_bias.astype(F32))
    tile = lambda w: pl.BlockSpec((1, tm, w), lambda b, i: (b, i, 0))
    whole = lambda a: pl.BlockSpec(a.shape, lambda b, i: (0,) * a.ndim)
    per_batch = pl.BlockSpec((1, 1, d), lambda b, i: (b, 0, 0))
    return pl.pallas_call(
        _oproj_kernel,
        grid=(bsz, seq // tm),
        in_specs=[tile(W_SB), tile(W_FOX), tile(W_DIL), whole(w1), whole(w2), whole(w3), tile(d),
                  per_batch, per_batch, per_batch,
                  pl.BlockSpec((1, d), lambda b, i: (0, 0)), pl.BlockSpec((1, d), lambda b, i: (0, 0)),
                  whole(wr), whole(rb)],
        out_specs=[tile(d), tile(d), tile(8), tile(8)],
        out_shape=[jax.ShapeDtypeStruct((bsz, seq, d), F32), jax.ShapeDtypeStruct((bsz, seq, d), F32),
                   jax.ShapeDtypeStruct((bsz, seq, 8), I32), jax.ShapeDtypeStruct((bsz, seq, 8), F32)],
        compiler_params=_params(("arbitrary", "arbitrary")),
        name="output_projection_router",
    )(o_sb, o_fx, o_dl, w1, w2, w3, x, gate1, scale2, shift2, ln_g.reshape(1, d), ln_b.reshape(1, d),
      wr, rb)


def _dispatch_plan(idx, gates, rows_per_block):
    n_tok = idx.shape[0]
    n_assign = n_tok * TOP_K
    flat_e = idx.reshape(-1)
    flat_w = gates.reshape(-1)
    order = jnp.argsort(flat_e, stable=True).astype(I32)
    e_sorted = flat_e[order]
    counts = jnp.bincount(flat_e, length=N_EXPERTS).astype(I32)
    padded = (counts + rows_per_block - 1) // rows_per_block * rows_per_block
    start = jnp.cumsum(counts) - counts
    pad_end = jnp.cumsum(padded)
    pad_start = pad_end - padded
    dest = (pad_start[e_sorted] + jnp.arange(n_assign, dtype=I32) - start[e_sorted]).astype(I32)
    n_blocks = -(-(n_assign + N_EXPERTS * (rows_per_block - 1)) // rows_per_block)
    rows = n_blocks * rows_per_block
    row_tok = jnp.zeros((rows,), I32).at[dest].set(order // TOP_K)
    row_w = jnp.zeros((rows,), F32).at[dest].set(flat_w[order])
    block_e = jnp.searchsorted(pad_end, jnp.arange(n_blocks, dtype=I32) * rows_per_block, side='right')
    block_e = jnp.minimum(block_e, N_EXPERTS - 1).astype(I32)
    n_used = (pad_end[-1] // rows_per_block).astype(I32).reshape(1)
    dest_of = jnp.zeros((n_assign,), I32).at[order].set(dest).reshape(n_tok, TOP_K)
    return row_tok, row_w, block_e, n_used, dest_of, n_blocks


def _row_copy(src_hbm, row, dst, slot, r, sem):
    return pltpu.make_async_copy(src_hbm.at[pl.ds(row, 1), :], dst.at[slot, pl.ds(r, 1), :], sem.at[slot])


def _expert_kernel(be_ref, nu_ref, tok_cur, tok_next, w_ref, h_hbm, wg_ref, wu_ref, wd_ref, y_ref,
                   xbuf, sem, wg_bf, wu_bf, wd_bf, *, rows):
    i = pl.program_id(0)
    slot = lax.rem(i, 2)
    n_used = nu_ref[0]

    def start_gather(tok_ref, s):
        def body(r, _):
            _row_copy(h_hbm, tok_ref[0, 0, r], xbuf, s, r, sem).start()
            return 0
        lax.fori_loop(0, rows, body, 0)

    @pl.when(i == 0)
    def _():
        start_gather(tok_cur, 0)

    @pl.when(i + 1 < n_used)
    def _():
        start_gather(tok_next, 1 - slot)

    new_expert = (i == 0) | (be_ref[i] != be_ref[jnp.maximum(i - 1, 0)])

    @pl.when(new_expert & (i < n_used))
    def _():
        wg_bf[...] = wg_ref[0].astype(BF16)
        wu_bf[...] = wu_ref[0].astype(BF16)
        wd_bf[...] = wd_ref[0].astype(BF16)

    @pl.when(i < n_used)
    def _():
        def wait(r, _):
            _row_copy(h_hbm, 0, xbuf, slot, r, sem).wait()
            return 0
        lax.fori_loop(0, rows, wait, 0)
        xb = xbuf[slot].astype(BF16)
        hidden = _silu(_dot(xb, wg_bf[...])) * _dot(xb, wu_bf[...])
        y_ref[...] = _dot(hidden.astype(BF16), wd_bf[...]) * w_ref[...]

    @pl.when(i >= n_used)
    def _():
        y_ref[...] = jnp.zeros_like(y_ref)


def _grouped_experts(h2, row_tok, row_w, block_e, n_used, n_blocks, w_gate, w_up, w_down,
                     rows=MOE_ROWS):
    n_tok, d = h2.shape
    hid = w_gate.shape[-1]
    tok3 = row_tok.reshape(n_blocks, 1, rows)
    smem_rows = lambda shift: pl.BlockSpec(
        (1, 1, rows), lambda i, be, nu: (jnp.minimum(i + shift, n_blocks - 1), 0, 0),
        memory_space=pltpu.SMEM)
    grid_spec = pltpu.PrefetchScalarGridSpec(
        num_scalar_prefetch=2,
        grid=(n_blocks,),
        in_specs=[
            smem_rows(0), smem_rows(1),
            pl.BlockSpec((rows, 1), lambda i, be, nu: (i, 0)),
            pl.BlockSpec(memory_space=pl.ANY),
            pl.BlockSpec((1, d, hid), lambda i, be, nu: (be[i], 0, 0)),
            pl.BlockSpec((1, d, hid), lambda i, be, nu: (be[i], 0, 0)),
            pl.BlockSpec((1, hid, d), lambda i, be, nu: (be[i], 0, 0)),
        ],
        out_specs=pl.BlockSpec((rows, d), lambda i, be, nu: (i, 0)),
        scratch_shapes=[
            pltpu.VMEM((2, rows, d), F32),
            pltpu.SemaphoreType.DMA((2,)),
            pltpu.VMEM((d, hid), BF16),
            pltpu.VMEM((d, hid), BF16),
            pltpu.VMEM((hid, d), BF16),
        ],
    )
    return pl.pallas_call(
        functools.partial(_expert_kernel, rows=rows),
        grid_spec=grid_spec,
        out_shape=jax.ShapeDtypeStruct((n_blocks * rows, d), F32),
        compiler_params=_params(("arbitrary",)),
        name="grouped_experts",
    )(block_e, n_used, tok3, tok3, row_w.reshape(n_blocks * rows, 1), h2, w_gate, w_up, w_down)


def _combine_kernel(dest_cur, dest_next, y_hbm, h2_ref, x1_ref, gate_ref, lng_ref, lnb_ref,
                    wg_ref, wu_ref, wd_ref, o_ref, buf, sem, *, tm):
    i = pl.program_id(0)
    slot = lax.rem(i, 2)
    n_rows = TOP_K * tm

    def start_gather(dest_ref, s):
        def body(r, _):
            _row_copy(y_hbm, dest_ref[0, 0, r], buf, s, r, sem).start()
            return 0
        lax.fori_loop(0, n_rows, body, 0)

    @pl.when(i == 0)
    def _():
        start_gather(dest_cur, 0)

    @pl.when(i + 1 < pl.num_programs(0))
    def _():
        start_gather(dest_next, 1 - slot)

    hb = h2_ref[...].astype(BF16)
    hidden = _silu(_dot(hb, wg_ref[...])) * _dot(hb, wu_ref[...])
    ffn = _dot(hidden.astype(BF16), wd_ref[...])

    def wait(r, _):
        _row_copy(y_hbm, 0, buf, slot, r, sem).wait()
        return 0
    lax.fori_loop(0, n_rows, wait, 0)
    for k in range(TOP_K):
        ffn = ffn + buf[slot, k * tm:(k + 1) * tm, :]
    o_ref[...] = _layer_norm(DEEPNORM_ALPHA * x1_ref[...] + gate_ref[0] * ffn, lng_ref[...], lnb_ref[...])


def _combine(y, dest_of, h2, x1, gate2, ln_g, ln_b, ws_gate, ws_up, ws_down, seq, tm=COMBINE_TOKENS):
    n_tok, d = h2.shape
    n_tiles = n_tok // tm
    hid = ws_gate.shape[-1]
    dest3 = dest_of.reshape(n_tiles, tm, TOP_K).transpose(0, 2, 1).reshape(n_tiles, 1, TOP_K * tm)
    smem_rows = lambda shift: pl.BlockSpec(
        (1, 1, TOP_K * tm), lambda i: (jnp.minimum(i + shift, n_tiles - 1), 0, 0),
        memory_space=pltpu.SMEM)
    tile = pl.BlockSpec((tm, d), lambda i: (i, 0))
    row = pl.BlockSpec((1, d), lambda i: (0, 0))
    return pl.pallas_call(
        functools.partial(_combine_kernel, tm=tm),
        grid=(n_tiles,),
        in_specs=[
            smem_rows(0), smem_rows(1),
            pl.BlockSpec(memory_space=pl.ANY),
            tile, tile,
            pl.BlockSpec((1, 1, d), lambda i: (i * tm // seq, 0, 0)),
            row, row,
            pl.BlockSpec((d, hid), lambda i: (0, 0)),
            pl.BlockSpec((d, hid), lambda i: (0, 0)),
            pl.BlockSpec((hid, d), lambda i: (0, 0)),
        ],
        out_specs=tile,
        out_shape=jax.ShapeDtypeStruct((n_tok, d), F32),
        scratch_shapes=[pltpu.VMEM((2, TOP_K * tm, d), F32), pltpu.SemaphoreType.DMA((2,))],
        compiler_params=_params(("arbitrary",)),
        name="combine_shared_layernorm",
    )(dest3, dest3, y, h2, x1, gate2, ln_g.reshape(1, d), ln_b.reshape(1, d),
      ws_gate.astype(BF16), ws_up.astype(BF16), ws_down.astype(BF16))


def kernel(x, c, positions, w_ada, b_ada, w_in, b_forget, head_norm_g, w_out, ln1_g, ln1_b, w_router,
           router_bias, w_exp_gate, w_exp_up, w_exp_down, w_sh_gate, w_sh_up, w_sh_down, ln2_g, ln2_b):
    bsz, seq, d = x.shape
    n_tok = bsz * seq
    mod = _ada_modulation(c, w_ada, b_ada)
    cos_t, sin_t = _rope_tables(positions)
    n_a = 3 * (W_SB + W_FOX)
    n_qkv = n_a + 3 * W_DIL
    for l in range(DEPTH):
        shift1, scale1, gate1, shift2, scale2, gate2 = (
            mod[l, :, k * d:(k + 1) * d].reshape(bsz, 1, d) for k in range(6))
        w_a = w_in[l, :, :n_a].astype(BF16)
        w_f = jnp.zeros((d, LANES), F32).at[:, :N_HEADS_FOX].set(w_in[l, :, n_qkv:])
        w_b = jnp.concatenate([w_in[l, :, n_a:n_qkv], w_f], axis=1).astype(BF16)
        proj_a = _modulated_matmul(x, scale1, shift1, w_a, BF16, tm=512, tn=n_a // 3)
        proj_b = _modulated_matmul(x, scale1, shift1, w_b, F32, tm=512, tn=w_b.shape[1])
        g = head_norm_g[l]
        o_sb = _sb_attention(proj_a, g[:W_SB])
        cum_col, cum_row = _fox_prefix(proj_b, 3 * W_DIL // LANES, b_forget[l])
        cum_row = cum_row[:, :N_HEADS_FOX].reshape(bsz, N_HEADS_FOX // 2, 2, seq)
        o_fx = _fox_attention(proj_a, 3 * W_SB // LANES, cum_col, cum_row, g[W_SB:W_SB + W_FOX])
        o_dl = _dil_attention(proj_b, cos_t, sin_t, g[W_SB + W_FOX:])
        x1, h2, idx, gates = _output_projection(o_sb, o_fx, o_dl, w_out[l], x, gate1, scale2, shift2,
                                                ln1_g[l], ln1_b[l], w_router[l], router_bias[l])
        h2 = h2.reshape(n_tok, d)
        row_tok, row_w, block_e, n_used, dest_of, n_blocks = _dispatch_plan(
            idx.reshape(n_tok, 8)[:, :TOP_K], gates.reshape(n_tok, 8)[:, :TOP_K], MOE_ROWS)
        y = _grouped_experts(h2, row_tok, row_w, block_e, n_used, n_blocks,
                             w_exp_gate[l], w_exp_up[l], w_exp_down[l])
        x = _combine(y, dest_of, h2, x1.reshape(n_tok, d), gate2, ln2_g[l], ln2_b[l],
                     w_sh_gate[l], w_sh_up[l], w_sh_down[l], seq).reshape(bsz, seq, d)
    return x
```

```python
import functools

import jax
import jax.numpy as jnp
from jax import lax
from jax.experimental import pallas as pl
from jax.experimental.pallas import tpu as pltpu

F32 = jnp.float32
BF16 = jnp.bfloat16
I32 = jnp.int32

D_MODEL = 2048
DEPTH = 2
HEAD_DIM = 64
N_HEADS_SB = 8
N_HEADS_FOX = 12
N_HEADS_DIL = 12
W_SB = N_HEADS_SB * HEAD_DIM
W_FOX = N_HEADS_FOX * HEAD_DIM
W_DIL = N_HEADS_DIL * HEAD_DIM
DILATIONS = (1, 4, 16)
DIL_SPAN = 128
ROPE_THETA = 10000.0
N_EXPERTS = 64
TOP_K = 6
ROUTED_SCALE = 2.5
DEEPNORM_ALPHA = (2 * DEPTH) ** 0.25
LN_EPS = 1e-5
RMS_EPS = 1e-6
QK_SCALE = HEAD_DIM ** -0.5
NEG_INF = float("-inf")

LANES = 128
ATTN_BLOCK = 256
MOE_ROWS = 256
COMBINE_TOKENS = 128
VMEM_LIMIT = 56 * 1024 * 1024


def _params(semantics, vmem=VMEM_LIMIT):
    return pltpu.CompilerParams(dimension_semantics=semantics, vmem_limit_bytes=vmem)


def _dot(a, b):
    return jnp.dot(a, b, preferred_element_type=F32)


def _dot_nt(a, b):
    return lax.dot_general(a, b, (((1,), (1,)), ((), ())), preferred_element_type=F32)


def _split2(a):
    hi = a.astype(BF16)
    lo = (a - hi.astype(F32)).astype(BF16)
    return hi, lo


def _split3(a):
    hi = a.astype(BF16)
    r1 = a - hi.astype(F32)
    mid = r1.astype(BF16)
    lo = (r1 - mid.astype(F32)).astype(BF16)
    return hi, mid, lo


def _dot_f32(a, b):
    ah, al = _split2(a)
    bh, bl = _split2(b)
    return _dot(ah, bh) + _dot(ah, bl) + _dot(al, bh)


def _softplus(z):
    return jnp.maximum(z, 0.0) + jnp.log1p(jnp.exp(-jnp.abs(z)))


def _silu(z):
    return z * jax.nn.sigmoid(z)


def _layer_norm(r, g, b):
    mu = jnp.mean(r, axis=-1, keepdims=True)
    d = r - mu
    var = jnp.mean(d * d, axis=-1, keepdims=True)
    return d * lax.rsqrt(var + LN_EPS) * g + b


def _store_row_slabs(dst, value):
    n, d = value.shape
    slabs = d // LANES
    for c in range(slabs):
        dst[pl.ds(c, n, stride=slabs), :] = value[:, c * LANES:(c + 1) * LANES]


def _load_row_slabs(src, n, slabs, dtype):
    return jnp.concatenate(
        [src[pl.ds(c, n, stride=slabs), :].astype(dtype) for c in range(slabs)], axis=1)


def _pair_rms_norm(o, g, lane):
    left = lane < HEAD_DIM
    sq = o * o
    s0 = jnp.sum(jnp.where(left, sq, 0.0), axis=-1, keepdims=True)
    s1 = jnp.sum(jnp.where(left, 0.0, sq), axis=-1, keepdims=True)
    ms = jnp.where(left, s0, s1) * (1.0 / HEAD_DIM)
    return o * lax.rsqrt(ms + RMS_EPS) * g


def _ada_kernel(c_ref, w_ref, b_ref, o_ref):
    o_ref[0] = _dot_f32(_silu(c_ref[...]), w_ref[0]) + b_ref[0]


def _ada_modulation(c, w_ada, b_ada, tn=1024):
    depth, d, n = w_ada.shape
    bsz = c.shape[0]
    return pl.pallas_call(
        _ada_kernel,
        grid=(depth, n // tn),
        in_specs=[
            pl.BlockSpec((bsz, d), lambda l, j: (0, 0)),
            pl.BlockSpec((1, d, tn), lambda l, j: (l, 0, j)),
            pl.BlockSpec((1, 1, tn), lambda l, j: (l, 0, j)),
        ],
        out_specs=pl.BlockSpec((1, bsz, tn), lambda l, j: (l, 0, j)),
        out_shape=jax.ShapeDtypeStruct((depth, bsz, n), F32),
        compiler_params=_params(("arbitrary", "arbitrary")),
        name="ada_modulation",
    )(c, w_ada, b_ada.reshape(depth, 1, n))


def _rope_kernel(pos_ref, invf_ref, sign_ref, cos_ref, sin_ref):
    ang = pos_ref[0].astype(F32) * invf_ref[...]
    cos_ref[0] = jnp.cos(ang)
    sin_ref[0] = jnp.sin(ang) * sign_ref[...]


def _rope_tables(positions, ts=512):
    bsz, seq = positions.shape
    half = HEAD_DIM // 2
    inv_freq = ROPE_THETA ** (-jnp.arange(half, dtype=F32) / half)
    inv_lane = jnp.tile(inv_freq, LANES // half).reshape(1, LANES)
    sign = jnp.tile(jnp.concatenate([-jnp.ones((half,), F32), jnp.ones((half,), F32)]),
                    LANES // HEAD_DIM).reshape(1, LANES)
    tab = jax.ShapeDtypeStruct((bsz, seq, LANES), F32)
    return pl.pallas_call(
        _rope_kernel,
        grid=(bsz, seq // ts),
        in_specs=[
            pl.BlockSpec((1, ts, 1), lambda b, i: (b, i, 0)),
            pl.BlockSpec((1, LANES), lambda b, i: (0, 0)),
            pl.BlockSpec((1, LANES), lambda b, i: (0, 0)),
        ],
        out_specs=[pl.BlockSpec((1, ts, LANES), lambda b, i: (b, i, 0))] * 2,
        out_shape=[tab, tab],
        compiler_params=_params(("arbitrary", "arbitrary")),
        name="rope_tables",
    )(positions.reshape(bsz, seq, 1), inv_lane, sign)


def _modmm_kernel(x_ref, sc_ref, sh_ref, w_ref, o_ref):
    h = x_ref[0] * (1.0 + sc_ref[0]) + sh_ref[0]
    o_ref[0] = _dot(h.astype(BF16), w_ref[...]).astype(o_ref.dtype)


def _modulated_matmul(x, scale, shift, w, out_dtype, tm, tn):
    bsz, seq, d = x.shape
    n = w.shape[1]
    return pl.pallas_call(
        _modmm_kernel,
        grid=(n // tn, bsz, seq // tm),
        in_specs=[
            pl.BlockSpec((1, tm, d), lambda j, b, i: (b, i, 0)),
            pl.BlockSpec((1, 1, d), lambda j, b, i: (b, 0, 0)),
            pl.BlockSpec((1, 1, d), lambda j, b, i: (b, 0, 0)),
            pl.BlockSpec((d, tn), lambda j, b, i: (0, j)),
        ],
        out_specs=pl.BlockSpec((1, tm, tn), lambda j, b, i: (b, i, j)),
        out_shape=jax.ShapeDtypeStruct((bsz, seq, n), out_dtype),
        compiler_params=_params(("arbitrary", "arbitrary", "arbitrary")),
        name="modulated_projection",
    )(x, scale, shift, w)


def _sb_kernel(q_ref, k_ref, v_ref, g_ref, o_ref, *, blk):
    qi = pl.program_id(2)
    lane = lax.broadcasted_iota(I32, (1, LANES), 1)
    row = lax.broadcasted_iota(I32, (blk, blk), 0)
    col = lax.broadcasted_iota(I32, (blk, blk), 1)
    past = col < row
    later_keys = jnp.where(row > col, 1.0, 0.0).astype(BF16)
    q = q_ref[0] * QK_SCALE
    q_heads = (jnp.where(lane < HEAD_DIM, q, jnp.zeros_like(q)),
               jnp.where(lane < HEAD_DIM, jnp.zeros_like(q), q))

    def block(kj, carry, masked):
        start = pl.multiple_of(kj * blk, blk)
        kb = k_ref[0, pl.ds(start, blk), :]
        vb = v_ref[0, pl.ds(start, blk), :]
        out = []
        for h in range(2):
            o, tail = carry[h]
            z = _dot_nt(q_heads[h], kb)
            log_keep = -jnp.log(1.0 + jnp.exp(-jnp.abs(z))) - jnp.maximum(z, 0.0)
            if masked:
                log_keep = jnp.where(past, log_keep, 0.0)
            hi, lo = _split2(log_keep)
            later = _dot(hi, later_keys) + _dot(lo, later_keys) + tail
            w = jnp.exp(z + log_keep + later)
            if masked:
                w = jnp.where(past, w, 0.0)
            o = o + _dot(w.astype(BF16), vb)
            tail = tail + jnp.sum(log_keep, axis=-1, keepdims=True)
            out.append((o, tail))
        return tuple(out)

    init = (jnp.zeros((blk, LANES), F32), jnp.zeros((blk, 1), F32))
    carry = block(qi, (init, init), True)
    carry = lax.fori_loop(0, qi, lambda it, c: block(qi - 1 - it, c, False), carry)
    o = jnp.where(lane < HEAD_DIM, carry[0][0], carry[1][0])
    o_ref[0] = _pair_rms_norm(o, g_ref[...], lane).astype(o_ref.dtype)


def _sb_attention(proj, gain, blk=ATTN_BLOCK):
    bsz, seq, _ = proj.shape
    pairs = W_SB // LANES
    return pl.pallas_call(
        functools.partial(_sb_kernel, blk=blk),
        grid=(bsz, pairs, seq // blk),
        in_specs=[
            pl.BlockSpec((1, blk, LANES), lambda b, p, i: (b, i, p)),
            pl.BlockSpec((1, seq, LANES), lambda b, p, i: (b, 0, pairs + p)),
            pl.BlockSpec((1, seq, LANES), lambda b, p, i: (b, 0, 2 * pairs + p)),
            pl.BlockSpec((1, LANES), lambda b, p, i: (0, p)),
        ],
        out_specs=pl.BlockSpec((1, blk, LANES), lambda b, p, i: (b, i, p)),
        out_shape=jax.ShapeDtypeStruct((bsz, seq, W_SB), BF16),
        compiler_params=_params(("arbitrary", "arbitrary", "arbitrary")),
        name="stick_breaking_attention",
    )(proj, proj, proj, gain.reshape(1, W_SB))


def _fox_prep_kernel(f_ref, b_ref, col_ref, row_ref, *, blk):
    seq = f_ref.shape[1]
    x = f_ref[0] + b_ref[...]
    log_f = -_softplus(-x)
    r = lax.broadcasted_iota(I32, (blk, blk), 0)
    c = lax.broadcasted_iota(I32, (blk, blk), 1)
    upto = jnp.where(c <= r, 1.0, 0.0).astype(BF16)
    carry = jnp.zeros((1, LANES), F32)
    parts = []
    for i in range(seq // blk):
        hi, mid, lo = _split3(log_f[i * blk:(i + 1) * blk])
        cs = _dot(upto, hi) + _dot(upto, mid) + _dot(upto, lo) + carry
        parts.append(cs)
        carry = cs[blk - 1:blk, :]
    cum = jnp.concatenate(parts, axis=0)
    col_ref[0] = cum
    row_ref[0] = cum.T


def _fox_prefix(f_logits, f_block, b_forget):
    bsz, seq, _ = f_logits.shape
    bias = jnp.zeros((1, LANES), F32).at[0, :N_HEADS_FOX].set(b_forget)
    return pl.pallas_call(
        functools.partial(_fox_prep_kernel, blk=256),
        grid=(bsz,),
        in_specs=[
            pl.BlockSpec((1, seq, LANES), lambda b: (b, 0, f_block)),
            pl.BlockSpec((1, LANES), lambda b: (0, 0)),
        ],
        out_specs=[
            pl.BlockSpec((1, seq, LANES), lambda b: (b, 0, 0)),
            pl.BlockSpec((1, LANES, seq), lambda b: (b, 0, 0)),
        ],
        out_shape=[jax.ShapeDtypeStruct((bsz, seq, LANES), F32),
                   jax.ShapeDtypeStruct((bsz, LANES, seq), F32)],
        compiler_params=_params(("arbitrary",)),
        name="forget_prefix",
    )(f_logits, bias)


def _fox_kernel(q_ref, k_ref, v_ref, cq_ref, ck_ref, g_ref, o_ref, *, blk):
    pair = pl.program_id(1)
    qi = pl.program_id(2)
    lane = lax.broadcasted_iota(I32, (1, LANES), 1)
    row = lax.broadcasted_iota(I32, (blk, blk), 0)
    col = lax.broadcasted_iota(I32, (blk, blk), 1)
    causal = col <= row
    q = q_ref[0] * QK_SCALE
    q_heads = (jnp.where(lane < HEAD_DIM, q, jnp.zeros_like(q)),
               jnp.where(lane < HEAD_DIM, jnp.zeros_like(q), q))
    cum_q_all = cq_ref[0]
    cum_q = tuple(jnp.sum(jnp.where(lane == 2 * pair + h, cum_q_all, 0.0), axis=-1, keepdims=True)
                  for h in range(2))

    def block(kj, carry, masked):
        start = pl.multiple_of(kj * blk, blk)
        kb = k_ref[0, pl.ds(start, blk), :]
        vb = v_ref[0, pl.ds(start, blk), :]
        out = []
        for h in range(2):
            m, l, o = carry[h]
            cum_k = ck_ref[0, 0, h:h + 1, pl.ds(start, blk)]
            z = _dot_nt(q_heads[h], kb) + (cum_q[h] - cum_k)
            if masked:
                z = jnp.where(causal, z, NEG_INF)
            m_new = jnp.maximum(m, jnp.max(z, axis=-1, keepdims=True))
            p = jnp.exp(z - m_new)
            a = jnp.exp(m - m_new)
            l = a * l + jnp.sum(p, axis=-1, keepdims=True)
            o = a * o + _dot(p.astype(BF16), vb)
            out.append((m_new, l, o))
        return tuple(out)

    init = (jnp.full((blk, 1), NEG_INF, F32), jnp.zeros((blk, 1), F32), jnp.zeros((blk, LANES), F32))
    carry = block(qi, (init, init), True)
    carry = lax.fori_loop(0, qi, lambda kj, c: block(kj, c, False), carry)
    o = jnp.where(lane < HEAD_DIM, carry[0][2] / carry[0][1], carry[1][2] / carry[1][1])
    o_ref[0] = _pair_rms_norm(o, g_ref[...], lane).astype(o_ref.dtype)


def _fox_attention(proj, col_base, cum_col, cum_row, gain, blk=ATTN_BLOCK):
    bsz, seq, _ = proj.shape
    pairs = W_FOX // LANES
    return pl.pallas_call(
        functools.partial(_fox_kernel, blk=blk),
        grid=(bsz, pairs, seq // blk),
        in_specs=[
            pl.BlockSpec((1, blk, LANES), lambda b, p, i: (b, i, col_base + p)),
            pl.BlockSpec((1, seq, LANES), lambda b, p, i: (b, 0, col_base + pairs + p)),
            pl.BlockSpec((1, seq, LANES), lambda b, p, i: (b, 0, col_base + 2 * pairs + p)),
            pl.BlockSpec((1, blk, LANES), lambda b, p, i: (b, i, 0)),
            pl.BlockSpec((1, 1, 2, seq), lambda b, p, i: (b, p, 0, 0)),
            pl.BlockSpec((1, LANES), lambda b, p, i: (0, p)),
        ],
        out_specs=pl.BlockSpec((1, blk, LANES), lambda b, p, i: (b, i, p)),
        out_shape=jax.ShapeDtypeStruct((bsz, seq, W_FOX), BF16),
        compiler_params=_params(("arbitrary", "arbitrary", "arbitrary")),
        name="forgetting_attention",
    )(proj, proj, proj, cum_col, cum_row, gain.reshape(1, W_FOX))


def _dil_kernel(q_ref, k_ref, v_ref, cos_ref, sin_ref, g_ref, o_ref,
                qs, ks, vs, og, lg, *, seq, pad):
    lane = lax.broadcasted_iota(I32, (1, LANES), 1)
    first_half = (lane & (HEAD_DIM - 1)) < HEAD_DIM // 2
    cos = cos_ref[0]
    sin = sin_ref[0]

    def rope(t):
        partner = jnp.where(first_half, pltpu.roll(t, LANES - HEAD_DIM // 2, 1),
                            pltpu.roll(t, HEAD_DIM // 2, 1))
        return t * cos + partner * sin

    qs[...] = rope(q_ref[0]) * QK_SCALE
    ks[0:pad, :] = jnp.zeros((pad, LANES), F32)
    vs[0:pad, :] = jnp.zeros((pad, LANES), F32)
    ks[pad:pad + seq, :] = rope(k_ref[0])
    vs[pad:pad + seq, :] = v_ref[0]

    span = DIL_SPAN
    qpos = lax.broadcasted_iota(I32, (span, 2 * span), 0)
    kpos = lax.broadcasted_iota(I32, (span, 2 * span), 1)
    band = (kpos >= qpos) & (kpos <= qpos + span)

    for gi, dil in enumerate(DILATIONS):
        nblk = seq // dil // span

        def body(idx, _, gi=gi, dil=dil):
            r = lax.rem(idx, dil)
            n = lax.div(idx, dil)
            q0 = r + n * (span * dil)
            k0 = pad + q0 - span * dil
            if dil == 1:
                q_rows = pl.ds(q0, span)
                k_rows = pl.ds(k0, 2 * span)
            else:
                q_rows = pl.ds(q0, span, stride=dil)
                k_rows = pl.ds(k0, 2 * span, stride=dil)
            qb = qs[q_rows, :]
            kb = ks[k_rows, :].astype(BF16)
            vb = vs[k_rows, :].astype(BF16)
            valid = band & (kpos + (n - 1) * span >= 0)
            outs, lses = [], []
            for h in range(2):
                in_head = (lane < HEAD_DIM) if h == 0 else (lane >= HEAD_DIM)
                qh = jnp.where(in_head, qb, 0.0).astype(BF16)
                z = jnp.where(valid, _dot_nt(qh, kb), NEG_INF)
                m = jnp.max(z, axis=-1, keepdims=True)
                p = jnp.exp(z - m)
                den = jnp.sum(p, axis=-1, keepdims=True)
                outs.append(_dot(p.astype(BF16), vb) / den)
                lses.append(m + jnp.log(den))
            og[gi, q_rows, :] = jnp.where(lane < HEAD_DIM, outs[0], outs[1])
            lg[gi, q_rows, :] = jnp.where(lane < HEAD_DIM, lses[0], lses[1])
            return 0

        lax.fori_loop(0, dil * nblk, body, 0, unroll=2)

    lse = lg[...]
    top = jnp.max(lse, axis=0)
    wts = jnp.exp(lse - top[None])
    o = jnp.sum(wts * og[...], axis=0) / jnp.sum(wts, axis=0)
    o_ref[0] = _pair_rms_norm(o, g_ref[...], lane).astype(o_ref.dtype)


def _dil_attention(proj, cos_t, sin_t, gain):
    bsz, seq, _ = proj.shape
    pairs = W_DIL // LANES
    pad = DIL_SPAN * max(DILATIONS)
    assert all(seq % (d * DIL_SPAN) == 0 for d in DILATIONS)
    return pl.pallas_call(
        functools.partial(_dil_kernel, seq=seq, pad=pad),
        grid=(bsz, pairs),
        in_specs=[
            pl.BlockSpec((1, seq, LANES), lambda b, p: (b, 0, p)),
            pl.BlockSpec((1, seq, LANES), lambda b, p: (b, 0, pairs + p)),
            pl.BlockSpec((1, seq, LANES), lambda b, p: (b, 0, 2 * pairs + p)),
            pl.BlockSpec((1, seq, LANES), lambda b, p: (b, 0, 0)),
            pl.BlockSpec((1, seq, LANES), lambda b, p: (b, 0, 0)),
            pl.BlockSpec((1, LANES), lambda b, p: (0, p)),
        ],
        out_specs=pl.BlockSpec((1, seq, LANES), lambda b, p: (b, 0, p)),
        out_shape=jax.ShapeDtypeStruct((bsz, seq, W_DIL), BF16),
        scratch_shapes=[
            pltpu.VMEM((seq, LANES), F32),
            pltpu.VMEM((pad + seq, LANES), F32),
            pltpu.VMEM((pad + seq, LANES), F32),
            pltpu.VMEM((len(DILATIONS), seq, LANES), F32),
            pltpu.VMEM((len(DILATIONS), seq, LANES), F32),
        ],
        compiler_params=_params(("arbitrary", "arbitrary")),
        name="dilated_attention",
    )(proj, proj, proj, cos_t, sin_t, gain.reshape(1, W_DIL))


def _oproj_kernel(osb_ref, ofx_ref, odl_ref, w1_ref, w2_ref, w3_ref, x_ref, gate_ref, sc_ref, sh_ref,
                  lng_ref, lnb_ref, wr_ref, rb_ref, x1_ref, h2_ref, h2s_ref, idx_ref, gw_ref, rank_ref,
                  cnt_ref, seen_ref):
    @pl.when((pl.program_id(0) == 0) & (pl.program_id(1) == 0))
    def _():
        seen_ref[...] = jnp.zeros_like(seen_ref)

    mix = (_dot(osb_ref[0], w1_ref[...]) + _dot(ofx_ref[0], w2_ref[...])
           + _dot(odl_ref[0], w3_ref[...]))
    x1 = _layer_norm(DEEPNORM_ALPHA * x_ref[0] + gate_ref[0] * mix, lng_ref[...], lnb_ref[...])
    x1_ref[0] = x1
    h2 = x1 * (1.0 + sc_ref[0]) + sh_ref[0]
    h2_ref[0] = h2
    _store_row_slabs(h2s_ref, h2)

    scores = jax.nn.sigmoid(_dot_f32(h2, wr_ref[...]))
    tm = scores.shape[0]
    lane = lax.broadcasted_iota(I32, (1, LANES), 1)
    lane_f = lane.astype(F32)
    sel = jnp.where(lane < N_EXPERTS, scores + rb_ref[...], NEG_INF)
    idx_acc = jnp.zeros((tm, LANES), I32)
    gate_acc = jnp.zeros((tm, LANES), F32)
    total = jnp.zeros((tm, 1), F32)
    onehot = jnp.zeros((tm, LANES), F32)
    picks = []
    for k in range(TOP_K):
        best = jnp.max(sel, axis=-1, keepdims=True)
        pick = jnp.min(jnp.where(sel == best, lane_f, float(LANES)), axis=-1, keepdims=True)
        chosen = lane_f == pick
        s = jnp.sum(jnp.where(chosen, scores, 0.0), axis=-1, keepdims=True)
        sel = jnp.where(chosen, NEG_INF, sel)
        idx_acc = jnp.where(lane == k, pick.astype(I32), idx_acc)
        gate_acc = jnp.where(lane == k, s, gate_acc)
        onehot = jnp.where(chosen, 1.0, onehot)
        total = total + s
        picks.append(pick)
    idx_ref[0] = idx_acc[:, :8]
    gw_ref[0] = (gate_acc / total * ROUTED_SCALE)[:, :8]

    r = lax.broadcasted_iota(I32, (tm, tm), 0)
    c = lax.broadcasted_iota(I32, (tm, tm), 1)
    earlier = jnp.where(c < r, 1.0, 0.0).astype(BF16)
    rank = _dot(earlier, onehot.astype(BF16)) + seen_ref[...]
    rank_acc = jnp.zeros((tm, LANES), F32)
    for k in range(TOP_K):
        rk = jnp.sum(jnp.where(lane_f == picks[k], rank, 0.0), axis=-1, keepdims=True)
        rank_acc = jnp.where(lane == k, rk, rank_acc)
    rank_ref[0] = rank_acc[:, :8].astype(I32)
    seen_ref[...] = seen_ref[...] + jnp.sum(onehot, axis=0, keepdims=True)
    cnt_ref[...] = seen_ref[...]


def _output_projection(o_sb, o_fx, o_dl, w_out, x, gate1, scale2, shift2, ln_g, ln_b, w_router,
                       router_bias, tm=256):
    bsz, seq, d = x.shape
    slabs = d // LANES
    w1 = w_out[:W_SB].astype(BF16)
    w2 = w_out[W_SB:W_SB + W_FOX].astype(BF16)
    w3 = w_out[W_SB + W_FOX:].astype(BF16)
    wr = jnp.zeros((d, LANES), F32).at[:, :N_EXPERTS].set(w_router.astype(F32))
    rb = jnp.zeros((1, LANES), F32).at[0, :N_EXPERTS].set(router_bias.astype(F32))
    tile = lambda w: pl.BlockSpec((1, tm, w), lambda b, i: (b, i, 0))
    whole = lambda a: pl.BlockSpec(a.shape, lambda b, i: (0,) * a.ndim)
    per_batch = pl.BlockSpec((1, 1, d), lambda b, i: (b, 0, 0))
    return pl.pallas_call(
        _oproj_kernel,
        grid=(bsz, seq // tm),
        in_specs=[tile(W_SB), tile(W_FOX), tile(W_DIL), whole(w1), whole(w2), whole(w3), tile(d),
                  per_batch, per_batch, per_batch,
                  pl.BlockSpec((1, d), lambda b, i: (0, 0)), pl.BlockSpec((1, d), lambda b, i: (0, 0)),
                  whole(wr), whole(rb)],
        out_specs=[tile(d), tile(d),
                   pl.BlockSpec((tm * slabs, LANES), lambda b, i: (b * (seq // tm) + i, 0)),
                   tile(8), tile(8), tile(8),
                   pl.BlockSpec((1, LANES), lambda b, i: (0, 0))],
        out_shape=[jax.ShapeDtypeStruct((bsz, seq, d), F32), jax.ShapeDtypeStruct((bsz, seq, d), F32),
                   jax.ShapeDtypeStruct((bsz * seq * slabs, LANES), F32),
                   jax.ShapeDtypeStruct((bsz, seq, 8), I32), jax.ShapeDtypeStruct((bsz, seq, 8), F32),
                   jax.ShapeDtypeStruct((bsz, seq, 8), I32), jax.ShapeDtypeStruct((1, LANES), F32)],
        scratch_shapes=[pltpu.VMEM((1, LANES), F32)],
        compiler_params=_params(("arbitrary", "arbitrary")),
        name="output_projection_router",
    )(o_sb, o_fx, o_dl, w1, w2, w3, x, gate1, scale2, shift2, ln_g.reshape(1, d), ln_b.reshape(1, d),
      wr, rb)


def _dispatch_plan(idx, rank, counts, rows_per_block):
    n_tok = idx.shape[0]
    n_assign = n_tok * TOP_K
    padded = (counts + rows_per_block - 1) // rows_per_block * rows_per_block
    pad_end = jnp.cumsum(padded)
    pad_start = pad_end - padded
    dest_of = (jnp.take(pad_start, idx) + rank).astype(I32)
    n_blocks = -(-(n_assign + N_EXPERTS * (rows_per_block - 1)) // rows_per_block)
    rows = n_blocks * rows_per_block
    tok = jnp.broadcast_to(jnp.arange(n_tok, dtype=I32)[:, None], (n_tok, TOP_K))
    row_tok = jnp.zeros((rows,), I32).at[dest_of.reshape(-1)].set(tok.reshape(-1))
    block_first_row = jnp.arange(n_blocks, dtype=I32) * rows_per_block
    block_e = jnp.sum((pad_end[None, :] <= block_first_row[:, None]).astype(I32), axis=1)
    block_e = jnp.minimum(block_e, N_EXPERTS - 1).astype(I32)
    n_used = (pad_end[-1] // rows_per_block).astype(I32).reshape(1)
    return row_tok, block_e, n_used, dest_of, n_blocks


def _row_copy(src_hbm, first, dst, slot, r, sem, slabs):
    return pltpu.make_async_copy(src_hbm.at[pl.ds(first, slabs), :],
                                 dst.at[slot, pl.ds(r * slabs, slabs), :], sem.at[slot])


def _expert_kernel(be_ref, nu_ref, tok_cur, tok_next, h_hbm, wg_ref, wu_ref, wd_ref, y_ref,
                   xbuf, sem, wg_bf, wu_bf, wd_bf, *, rows, slabs):
    i = pl.program_id(0)
    slot = lax.rem(i, 2)
    n_used = nu_ref[0]

    def start_gather(tok_ref, s):
        def body(r, _):
            _row_copy(h_hbm, tok_ref[0, 0, r], xbuf, s, r, sem, slabs).start()
            return 0
        lax.fori_loop(0, rows, body, 0, unroll=8)

    @pl.when(i == 0)
    def _():
        start_gather(tok_cur, 0)

    @pl.when(i + 1 < n_used)
    def _():
        start_gather(tok_next, 1 - slot)

    new_expert = (i == 0) | (be_ref[i] != be_ref[jnp.maximum(i - 1, 0)])

    @pl.when(new_expert & (i < n_used))
    def _():
        wg_bf[...] = wg_ref[0, 0].astype(BF16)
        wu_bf[...] = wu_ref[0, 0].astype(BF16)
        wd_bf[...] = wd_ref[0, 0].astype(BF16)

    @pl.when(i < n_used)
    def _():
        def wait(r, _):
            _row_copy(h_hbm, 0, xbuf, slot, r, sem, slabs).wait()
            return 0
        lax.fori_loop(0, rows, wait, 0, unroll=8)
        xb = _load_row_slabs(xbuf.at[slot], rows, slabs, BF16)
        hidden = _silu(_dot(xb, wg_bf[...])) * _dot(xb, wu_bf[...])
        _store_row_slabs(y_ref, _dot(hidden.astype(BF16), wd_bf[...]))

    @pl.when(i >= n_used)
    def _():
        y_ref[...] = jnp.zeros_like(y_ref)


def _grouped_experts(h2, row_tok, block_e, n_used, n_blocks, layer, w_gate, w_up, w_down,
                     rows=MOE_ROWS):
    d, hid = w_gate.shape[-2:]
    slabs = d // LANES
    tok3 = (row_tok * slabs).reshape(n_blocks, 1, rows)
    smem_rows = lambda shift: pl.BlockSpec(
        (1, 1, rows), lambda i, be, nu: (jnp.minimum(i + shift, n_blocks - 1), 0, 0),
        memory_space=pltpu.SMEM)
    grid_spec = pltpu.PrefetchScalarGridSpec(
        num_scalar_prefetch=2,
        grid=(n_blocks,),
        in_specs=[
            smem_rows(0), smem_rows(1),
            pl.BlockSpec(memory_space=pl.ANY),
            pl.BlockSpec((1, 1, d, hid), lambda i, be, nu: (layer, be[i], 0, 0)),
            pl.BlockSpec((1, 1, d, hid), lambda i, be, nu: (layer, be[i], 0, 0)),
            pl.BlockSpec((1, 1, hid, d), lambda i, be, nu: (layer, be[i], 0, 0)),
        ],
        out_specs=pl.BlockSpec((rows * slabs, LANES), lambda i, be, nu: (i, 0)),
        scratch_shapes=[
            pltpu.VMEM((2, rows * slabs, LANES), F32),
            pltpu.SemaphoreType.DMA((2,)),
            pltpu.VMEM((d, hid), BF16),
            pltpu.VMEM((d, hid), BF16),
            pltpu.VMEM((hid, d), BF16),
        ],
    )
    return pl.pallas_call(
        functools.partial(_expert_kernel, rows=rows, slabs=slabs),
        grid_spec=grid_spec,
        out_shape=jax.ShapeDtypeStruct((n_blocks * rows * slabs, LANES), F32),
        compiler_params=_params(("arbitrary",)),
        name="grouped_experts",
    )(block_e, n_used, tok3, tok3, h2, w_gate, w_up, w_down)


def _combine_kernel(dest_cur, dest_next, y_hbm, gw_ref, h2_ref, x1_ref, gate_ref, lng_ref, lnb_ref,
                    wg_ref, wu_ref, wd_ref, o_ref, buf, routed, sem, *, tm, slabs):
    i = pl.program_id(0)
    slot = lax.rem(i, 2)
    n_rows = TOP_K * tm

    def start_gather(dest_ref, s):
        def body(r, _):
            _row_copy(y_hbm, dest_ref[0, 0, r], buf, s, r, sem, slabs).start()
            return 0
        lax.fori_loop(0, n_rows, body, 0, unroll=8)

    @pl.when(i == 0)
    def _():
        start_gather(dest_cur, 0)

    @pl.when(i + 1 < pl.num_programs(0))
    def _():
        start_gather(dest_next, 1 - slot)

    hb = h2_ref[...].astype(BF16)
    hidden = _silu(_dot(hb, wg_ref[...])) * _dot(hb, wu_ref[...])
    ffn = _dot(hidden.astype(BF16), wd_ref[...])

    def wait(r, _):
        _row_copy(y_hbm, 0, buf, slot, r, sem, slabs).wait()
        return 0
    lax.fori_loop(0, n_rows, wait, 0, unroll=8)
    gw = gw_ref[...]
    span = tm * slabs
    acc = gw[:, 0:1] * buf[slot, 0:span, :]
    for k in range(1, TOP_K):
        acc = acc + gw[:, k:k + 1] * buf[slot, k * span:(k + 1) * span, :]
    routed[...] = acc
    ffn = ffn + _load_row_slabs(routed, tm, slabs, F32)
    o_ref[...] = _layer_norm(DEEPNORM_ALPHA * x1_ref[...] + gate_ref[0] * ffn, lng_ref[...], lnb_ref[...])


def _combine(y, dest_of, gates, h2, x1, gate2, ln_g, ln_b, ws_gate, ws_up, ws_down, seq,
             tm=COMBINE_TOKENS):
    n_tok, d = h2.shape
    n_tiles = n_tok // tm
    hid = ws_gate.shape[-1]
    slabs = d // LANES
    dest3 = (dest_of * slabs).reshape(n_tiles, tm, TOP_K).transpose(0, 2, 1).reshape(n_tiles, 1, TOP_K * tm)
    gates_rep = jnp.repeat(gates, slabs, axis=0)
    smem_rows = lambda shift: pl.BlockSpec(
        (1, 1, TOP_K * tm), lambda i: (jnp.minimum(i + shift, n_tiles - 1), 0, 0),
        memory_space=pltpu.SMEM)
    tile = pl.BlockSpec((tm, d), lambda i: (i, 0))
    row = pl.BlockSpec((1, d), lambda i: (0, 0))
    return pl.pallas_call(
        functools.partial(_combine_kernel, tm=tm, slabs=slabs),
        grid=(n_tiles,),
        in_specs=[
            smem_rows(0), smem_rows(1),
            pl.BlockSpec(memory_space=pl.ANY),
            pl.BlockSpec((tm * slabs, 8), lambda i: (i, 0)),
            tile, tile,
            pl.BlockSpec((1, 1, d), lambda i: (i * tm // seq, 0, 0)),
            row, row,
            pl.BlockSpec((d, hid), lambda i: (0, 0)),
            pl.BlockSpec((d, hid), lambda i: (0, 0)),
            pl.BlockSpec((hid, d), lambda i: (0, 0)),
        ],
        out_specs=tile,
        out_shape=jax.ShapeDtypeStruct((n_tok, d), F32),
        scratch_shapes=[pltpu.VMEM((2, TOP_K * tm * slabs, LANES), F32),
                        pltpu.VMEM((tm * slabs, LANES), F32), pltpu.SemaphoreType.DMA((2,))],
        compiler_params=_params(("arbitrary",)),
        name="combine_shared_layernorm",
    )(dest3, dest3, y, gates_rep, h2, x1, gate2, ln_g.reshape(1, d), ln_b.reshape(1, d),
      ws_gate.astype(BF16), ws_up.astype(BF16), ws_down.astype(BF16))


def kernel(x, c, positions, w_ada, b_ada, w_in, b_forget, head_norm_g, w_out, ln1_g, ln1_b, w_router,
           router_bias, w_exp_gate, w_exp_up, w_exp_down, w_sh_gate, w_sh_up, w_sh_down, ln2_g, ln2_b):
    bsz, seq, d = x.shape
    n_tok = bsz * seq
    mod = _ada_modulation(c, w_ada, b_ada)
    cos_t, sin_t = _rope_tables(positions)
    n_a = 3 * (W_SB + W_FOX)
    n_qkv = n_a + 3 * W_DIL
    for l in range(DEPTH):
        shift1, scale1, gate1, shift2, scale2, gate2 = (
            mod[l, :, k * d:(k + 1) * d].reshape(bsz, 1, d) for k in range(6))
        w_a = w_in[l, :, :n_a].astype(BF16)
        w_f = jnp.zeros((d, LANES), F32).at[:, :N_HEADS_FOX].set(w_in[l, :, n_qkv:])
        w_b = jnp.concatenate([w_in[l, :, n_a:n_qkv], w_f], axis=1).astype(BF16)
        proj_a = _modulated_matmul(x, scale1, shift1, w_a, BF16, tm=512, tn=n_a // 3)
        proj_b = _modulated_matmul(x, scale1, shift1, w_b, F32, tm=512, tn=w_b.shape[1])
        g = head_norm_g[l]
        o_sb = _sb_attention(proj_a, g[:W_SB])
        cum_col, cum_row = _fox_prefix(proj_b, 3 * W_DIL // LANES, b_forget[l])
        cum_row = cum_row[:, :N_HEADS_FOX].reshape(bsz, N_HEADS_FOX // 2, 2, seq)
        o_fx = _fox_attention(proj_a, 3 * W_SB // LANES, cum_col, cum_row, g[W_SB:W_SB + W_FOX])
        o_dl = _dil_attention(proj_b, cos_t, sin_t, g[W_SB + W_FOX:])
        x1, h2, h2_slabs, idx, gates, rank, counts = _output_projection(
            o_sb, o_fx, o_dl, w_out[l], x, gate1, scale2, shift2, ln1_g[l], ln1_b[l], w_router[l],
            router_bias[l])
        h2 = h2.reshape(n_tok, d)
        row_tok, block_e, n_used, dest_of, n_blocks = _dispatch_plan(
            idx.reshape(n_tok, 8)[:, :TOP_K], rank.reshape(n_tok, 8)[:, :TOP_K],
            counts[0, :N_EXPERTS].astype(I32), MOE_ROWS)
        y = _grouped_experts(h2_slabs, row_tok, block_e, n_used, n_blocks, l,
                             w_exp_gate, w_exp_up, w_exp_down)
        x = _combine(y, dest_of, gates.reshape(n_tok, 8), h2, x1.reshape(n_tok, d), gate2, ln2_g[l],
                     ln2_b[l], w_sh_gate[l], w_sh_up[l], w_sh_down[l], seq).reshape(bsz, seq, d)
    return x
```

```python
import functools

import jax
import jax.numpy as jnp
from jax import lax
from jax.experimental import pallas as pl
from jax.experimental.pallas import tpu as pltpu

F32 = jnp.float32
BF16 = jnp.bfloat16
I32 = jnp.int32

D_MODEL = 2048
DEPTH = 2
HEAD_DIM = 64
N_HEADS_SB = 8
N_HEADS_FOX = 12
N_HEADS_DIL = 12
W_SB = N_HEADS_SB * HEAD_DIM
W_FOX = N_HEADS_FOX * HEAD_DIM
W_DIL = N_HEADS_DIL * HEAD_DIM
DILATIONS = (1, 4, 16)
DIL_SPAN = 128
ROPE_THETA = 10000.0
N_EXPERTS = 64
TOP_K = 6
ROUTED_SCALE = 2.5
DEEPNORM_ALPHA = (2 * DEPTH) ** 0.25
LN_EPS = 1e-5
RMS_EPS = 1e-6
QK_SCALE = HEAD_DIM ** -0.5
NEG_INF = float("-inf")

LANES = 128
ATTN_QUERY_BLOCK = 512
ATTN_KEY_BLOCK = 256
MOE_ROWS = 256
COMBINE_TOKENS = 128
VMEM_LIMIT = 56 * 1024 * 1024


def _params(semantics, vmem=VMEM_LIMIT):
    return pltpu.CompilerParams(dimension_semantics=semantics, vmem_limit_bytes=vmem)


def _dot(a, b):
    return jnp.dot(a, b, preferred_element_type=F32)


def _dot_nt(a, b):
    return lax.dot_general(a, b, (((1,), (1,)), ((), ())), preferred_element_type=F32)


def _split2(a):
    hi = a.astype(BF16)
    lo = (a - hi.astype(F32)).astype(BF16)
    return hi, lo


def _split3(a):
    hi = a.astype(BF16)
    r1 = a - hi.astype(F32)
    mid = r1.astype(BF16)
    lo = (r1 - mid.astype(F32)).astype(BF16)
    return hi, mid, lo


def _dot_f32(a, b):
    ah, al = _split2(a)
    bh, bl = _split2(b)
    return _dot(ah, bh) + _dot(ah, bl) + _dot(al, bh)


def _softplus(z):
    return jnp.maximum(z, 0.0) + jnp.log1p(jnp.exp(-jnp.abs(z)))


def _silu(z):
    return z * jax.nn.sigmoid(z)


def _layer_norm(r, g, b):
    mu = jnp.mean(r, axis=-1, keepdims=True)
    d = r - mu
    var = jnp.mean(d * d, axis=-1, keepdims=True)
    return d * lax.rsqrt(var + LN_EPS) * g + b


def _slab_pitch(d):
    return d // LANES + 1


def _store_row_slabs(dst, value):
    n, d = value.shape
    slabs = d // LANES
    pitch = _slab_pitch(d)
    for c in range(slabs):
        dst[pl.ds(c, n, stride=pitch), :] = value[:, c * LANES:(c + 1) * LANES]
    dst[pl.ds(slabs, n, stride=pitch), :] = jnp.zeros((n, LANES), dst.dtype)


def _load_row_slabs(src, n, d, dtype):
    pitch = _slab_pitch(d)
    return jnp.concatenate(
        [src[pl.ds(c, n, stride=pitch), :].astype(dtype) for c in range(d // LANES)], axis=1)


def _pair_rms_norm(o, g, lane):
    left = lane < HEAD_DIM
    sq = o * o
    s0 = jnp.sum(jnp.where(left, sq, 0.0), axis=-1, keepdims=True)
    s1 = jnp.sum(jnp.where(left, 0.0, sq), axis=-1, keepdims=True)
    ms = jnp.where(left, s0, s1) * (1.0 / HEAD_DIM)
    return o * lax.rsqrt(ms + RMS_EPS) * g


def _ada_kernel(c_ref, w_ref, b_ref, o_ref):
    o_ref[0] = _dot_f32(_silu(c_ref[...]), w_ref[0]) + b_ref[0]


def _ada_modulation(c, w_ada, b_ada, tn=1024):
    depth, d, n = w_ada.shape
    bsz = c.shape[0]
    return pl.pallas_call(
        _ada_kernel,
        grid=(depth, n // tn),
        in_specs=[
            pl.BlockSpec((bsz, d), lambda l, j: (0, 0)),
            pl.BlockSpec((1, d, tn), lambda l, j: (l, 0, j)),
            pl.BlockSpec((1, 1, tn), lambda l, j: (l, 0, j)),
        ],
        out_specs=pl.BlockSpec((1, bsz, tn), lambda l, j: (l, 0, j)),
        out_shape=jax.ShapeDtypeStruct((depth, bsz, n), F32),
        compiler_params=_params(("arbitrary", "arbitrary")),
        name="ada_modulation",
    )(c, w_ada, b_ada.reshape(depth, 1, n))


def _rope_kernel(pos_ref, invf_ref, sign_ref, cos_ref, sin_ref):
    ang = pos_ref[0].astype(F32) * invf_ref[...]
    cos_ref[0] = jnp.cos(ang)
    sin_ref[0] = jnp.sin(ang) * sign_ref[...]


def _rope_tables(positions, ts=512):
    bsz, seq = positions.shape
    half = HEAD_DIM // 2
    inv_freq = ROPE_THETA ** (-jnp.arange(half, dtype=F32) / half)
    inv_lane = jnp.tile(inv_freq, LANES // half).reshape(1, LANES)
    sign = jnp.tile(jnp.concatenate([-jnp.ones((half,), F32), jnp.ones((half,), F32)]),
                    LANES // HEAD_DIM).reshape(1, LANES)
    tab = jax.ShapeDtypeStruct((bsz, seq, LANES), F32)
    return pl.pallas_call(
        _rope_kernel,
        grid=(bsz, seq // ts),
        in_specs=[
            pl.BlockSpec((1, ts, 1), lambda b, i: (b, i, 0)),
            pl.BlockSpec((1, LANES), lambda b, i: (0, 0)),
            pl.BlockSpec((1, LANES), lambda b, i: (0, 0)),
        ],
        out_specs=[pl.BlockSpec((1, ts, LANES), lambda b, i: (b, i, 0))] * 2,
        out_shape=[tab, tab],
        compiler_params=_params(("arbitrary", "arbitrary")),
        name="rope_tables",
    )(positions.reshape(bsz, seq, 1), inv_lane, sign)


def _modmm_kernel(x_ref, sc_ref, sh_ref, w_ref, o_ref):
    h = x_ref[0] * (1.0 + sc_ref[0]) + sh_ref[0]
    o_ref[0] = _dot(h.astype(BF16), w_ref[...]).astype(o_ref.dtype)


def _modulated_matmul(x, scale, shift, w, out_dtype, tm, tn):
    bsz, seq, d = x.shape
    n = w.shape[1]
    return pl.pallas_call(
        _modmm_kernel,
        grid=(n // tn, bsz, seq // tm),
        in_specs=[
            pl.BlockSpec((1, tm, d), lambda j, b, i: (b, i, 0)),
            pl.BlockSpec((1, 1, d), lambda j, b, i: (b, 0, 0)),
            pl.BlockSpec((1, 1, d), lambda j, b, i: (b, 0, 0)),
            pl.BlockSpec((d, tn), lambda j, b, i: (0, j)),
        ],
        out_specs=pl.BlockSpec((1, tm, tn), lambda j, b, i: (b, i, j)),
        out_shape=jax.ShapeDtypeStruct((bsz, seq, n), out_dtype),
        compiler_params=_params(("arbitrary", "arbitrary", "arbitrary")),
        name="modulated_projection",
    )(x, scale, shift, w)


def _stack_pair_queries(q, lane):
    zero = jnp.zeros_like(q)
    return jnp.concatenate([jnp.where(lane < HEAD_DIM, q, zero), jnp.where(lane < HEAD_DIM, zero, q)], axis=0)


def _sb_kernel(q_ref, k_ref, v_ref, g_ref, o_ref, *, tq, tk):
    qi = pl.program_id(2)
    sub = tq // tk
    lane = lax.broadcasted_iota(I32, (1, LANES), 1)
    row = lax.broadcasted_iota(I32, (2 * tq, tk), 0) & (tq - 1)
    col = lax.broadcasted_iota(I32, (2 * tq, tk), 1)
    r2 = lax.broadcasted_iota(I32, (tk, tk), 0)
    c2 = lax.broadcasted_iota(I32, (tk, tk), 1)
    later_keys = jnp.where(r2 > c2, 1.0, 0.0).astype(BF16)
    q2 = _stack_pair_queries(q_ref[0] * QK_SCALE, lane)

    def block(kj, carry, diag):
        o, tail = carry
        start = pl.multiple_of(kj * tk, tk)
        kb = k_ref[0, pl.ds(start, tk), :]
        vb = v_ref[0, pl.ds(start, tk), :]
        z = _dot_nt(q2, kb)
        log_keep = -jnp.log(1.0 + jnp.exp(-jnp.abs(z))) - jnp.maximum(z, 0.0)
        if diag is not None:
            past = col + diag * tk < row
            log_keep = jnp.where(past, log_keep, 0.0)
        hi, lo = _split2(log_keep)
        sums = _dot(jnp.concatenate([hi, lo], axis=0), later_keys)
        later = sums[:2 * tq] + sums[2 * tq:] + tail
        w = jnp.exp(z + log_keep + later)
        if diag is not None:
            w = jnp.where(past, w, 0.0)
        o = o + _dot(w.astype(BF16), vb)
        tail = tail + jnp.sum(log_keep, axis=-1, keepdims=True)
        return o, tail

    carry = (jnp.zeros((2 * tq, LANES), F32), jnp.zeros((2 * tq, 1), F32))
    for diag in reversed(range(sub)):
        carry = block(qi * sub + diag, carry, diag)
    carry = lax.fori_loop(0, qi * sub, lambda it, c: block(qi * sub - 1 - it, c, None), carry)
    o = jnp.where(lane < HEAD_DIM, carry[0][:tq], carry[0][tq:])
    o_ref[0] = _pair_rms_norm(o, g_ref[...], lane).astype(o_ref.dtype)


def _sb_attention(proj, gain, blk=ATTN_QUERY_BLOCK):
    bsz, seq, _ = proj.shape
    pairs = W_SB // LANES
    return pl.pallas_call(
        functools.partial(_sb_kernel, tq=blk, tk=ATTN_KEY_BLOCK),
        grid=(bsz, pairs, seq // blk),
        in_specs=[
            pl.BlockSpec((1, blk, LANES), lambda b, p, i: (b, i, p)),
            pl.BlockSpec((1, seq, LANES), lambda b, p, i: (b, 0, pairs + p)),
            pl.BlockSpec((1, seq, LANES), lambda b, p, i: (b, 0, 2 * pairs + p)),
            pl.BlockSpec((1, LANES), lambda b, p, i: (0, p)),
        ],
        out_specs=pl.BlockSpec((1, blk, LANES), lambda b, p, i: (b, i, p)),
        out_shape=jax.ShapeDtypeStruct((bsz, seq, W_SB), BF16),
        compiler_params=_params(("arbitrary", "arbitrary", "arbitrary")),
        name="stick_breaking_attention",
    )(proj, proj, proj, gain.reshape(1, W_SB))


def _fox_prep_kernel(f_ref, b_ref, col_ref, row_ref, *, blk):
    seq = f_ref.shape[1]
    x = f_ref[0] + b_ref[...]
    log_f = -_softplus(-x)
    r = lax.broadcasted_iota(I32, (blk, blk), 0)
    c = lax.broadcasted_iota(I32, (blk, blk), 1)
    upto = jnp.where(c <= r, 1.0, 0.0).astype(BF16)
    carry = jnp.zeros((1, LANES), F32)
    parts = []
    for i in range(seq // blk):
        hi, mid, lo = _split3(log_f[i * blk:(i + 1) * blk])
        cs = _dot(upto, hi) + _dot(upto, mid) + _dot(upto, lo) + carry
        parts.append(cs)
        carry = cs[blk - 1:blk, :]
    cum = jnp.concatenate(parts, axis=0)
    col_ref[0] = cum
    row_ref[0] = cum.T


def _fox_prefix(f_logits, f_block, b_forget):
    bsz, seq, _ = f_logits.shape
    bias = jnp.zeros((1, LANES), F32).at[0, :N_HEADS_FOX].set(b_forget)
    return pl.pallas_call(
        functools.partial(_fox_prep_kernel, blk=256),
        grid=(bsz,),
        in_specs=[
            pl.BlockSpec((1, seq, LANES), lambda b: (b, 0, f_block)),
            pl.BlockSpec((1, LANES), lambda b: (0, 0)),
        ],
        out_specs=[
            pl.BlockSpec((1, seq, LANES), lambda b: (b, 0, 0)),
            pl.BlockSpec((1, LANES, seq), lambda b: (b, 0, 0)),
        ],
        out_shape=[jax.ShapeDtypeStruct((bsz, seq, LANES), F32),
                   jax.ShapeDtypeStruct((bsz, LANES, seq), F32)],
        compiler_params=_params(("arbitrary",)),
        name="forget_prefix",
    )(f_logits, bias)


def _fox_kernel(q_ref, k_ref, v_ref, cq_ref, ck_ref, g_ref, o_ref, *, tq, tk):
    pair = pl.program_id(1)
    qi = pl.program_id(2)
    sub = tq // tk
    lane = lax.broadcasted_iota(I32, (1, LANES), 1)
    row = lax.broadcasted_iota(I32, (2 * tq, tk), 0) & (tq - 1)
    col = lax.broadcasted_iota(I32, (2 * tq, tk), 1)
    q2 = _stack_pair_queries(q_ref[0] * QK_SCALE, lane)
    cum_q_all = cq_ref[0]
    cum_q = [jnp.sum(jnp.where(lane == 2 * pair + h, cum_q_all, 0.0), axis=-1, keepdims=True)
             for h in range(2)]

    def block(kj, carry, diag):
        m, l, o = carry
        start = pl.multiple_of(kj * tk, tk)
        kb = k_ref[0, pl.ds(start, tk), :]
        vb = v_ref[0, pl.ds(start, tk), :]
        qk = _dot_nt(q2, kb)
        z = jnp.concatenate(
            [qk[h * tq:(h + 1) * tq] + (cum_q[h] - ck_ref[0, 0, h:h + 1, pl.ds(start, tk)])
             for h in range(2)], axis=0)
        if diag is not None:
            z = jnp.where(col + diag * tk <= row, z, NEG_INF)
        m_new = jnp.maximum(m, jnp.max(z, axis=-1, keepdims=True))
        p = jnp.exp(z - m_new)
        a = jnp.exp(m - m_new)
        l = a * l + jnp.sum(p, axis=-1, keepdims=True)
        o = a * o + _dot(p.astype(BF16), vb)
        return m_new, l, o

    carry = (jnp.full((2 * tq, 1), NEG_INF, F32), jnp.zeros((2 * tq, 1), F32),
             jnp.zeros((2 * tq, LANES), F32))
    for diag in range(sub):
        carry = block(qi * sub + diag, carry, diag)
    carry = lax.fori_loop(0, qi * sub, lambda kj, c: block(kj, c, None), carry)
    o = carry[2] / carry[1]
    o = jnp.where(lane < HEAD_DIM, o[:tq], o[tq:])
    o_ref[0] = _pair_rms_norm(o, g_ref[...], lane).astype(o_ref.dtype)


def _fox_attention(proj, col_base, cum_col, cum_row, gain, blk=ATTN_QUERY_BLOCK):
    bsz, seq, _ = proj.shape
    pairs = W_FOX // LANES
    return pl.pallas_call(
        functools.partial(_fox_kernel, tq=blk, tk=ATTN_KEY_BLOCK),
        grid=(bsz, pairs, seq // blk),
        in_specs=[
            pl.BlockSpec((1, blk, LANES), lambda b, p, i: (b, i, col_base + p)),
            pl.BlockSpec((1, seq, LANES), lambda b, p, i: (b, 0, col_base + pairs + p)),
            pl.BlockSpec((1, seq, LANES), lambda b, p, i: (b, 0, col_base + 2 * pairs + p)),
            pl.BlockSpec((1, blk, LANES), lambda b, p, i: (b, i, 0)),
            pl.BlockSpec((1, 1, 2, seq), lambda b, p, i: (b, p, 0, 0)),
            pl.BlockSpec((1, LANES), lambda b, p, i: (0, p)),
        ],
        out_specs=pl.BlockSpec((1, blk, LANES), lambda b, p, i: (b, i, p)),
        out_shape=jax.ShapeDtypeStruct((bsz, seq, W_FOX), BF16),
        compiler_params=_params(("arbitrary", "arbitrary", "arbitrary")),
        name="forgetting_attention",
    )(proj, proj, proj, cum_col, cum_row, gain.reshape(1, W_FOX))


def _dil_kernel(q_ref, k_ref, v_ref, cos_ref, sin_ref, g_ref, o_ref,
                qs, ks, vs, og, lg, *, seq, pad):
    lane = lax.broadcasted_iota(I32, (1, LANES), 1)
    first_half = (lane & (HEAD_DIM - 1)) < HEAD_DIM // 2
    cos = cos_ref[0]
    sin = sin_ref[0]

    def rope(t):
        partner = jnp.where(first_half, pltpu.roll(t, LANES - HEAD_DIM // 2, 1),
                            pltpu.roll(t, HEAD_DIM // 2, 1))
        return t * cos + partner * sin

    qs[...] = rope(q_ref[0]) * QK_SCALE
    ks[0:pad, :] = jnp.zeros((pad, LANES), F32)
    vs[0:pad, :] = jnp.zeros((pad, LANES), F32)
    ks[pad:pad + seq, :] = rope(k_ref[0])
    vs[pad:pad + seq, :] = v_ref[0]

    span = DIL_SPAN
    qpos = lax.broadcasted_iota(I32, (2 * span, 2 * span), 0) & (span - 1)
    kpos = lax.broadcasted_iota(I32, (2 * span, 2 * span), 1)
    band = (kpos >= qpos) & (kpos <= qpos + span)

    for gi, dil in enumerate(DILATIONS):
        nblk = seq // dil // span

        def body(idx, _, gi=gi, dil=dil):
            r = lax.rem(idx, dil)
            n = lax.div(idx, dil)
            q0 = r + n * (span * dil)
            k0 = pad + q0 - span * dil
            if dil == 1:
                q_rows = pl.ds(q0, span)
                k_rows = pl.ds(k0, 2 * span)
            else:
                q_rows = pl.ds(q0, span, stride=dil)
                k_rows = pl.ds(k0, 2 * span, stride=dil)
            q2 = _stack_pair_queries(qs[q_rows, :].astype(BF16), lane)
            kb = ks[k_rows, :].astype(BF16)
            vb = vs[k_rows, :].astype(BF16)
            valid = band & (kpos + (n - 1) * span >= 0)
            z = jnp.where(valid, _dot_nt(q2, kb), NEG_INF)
            m = jnp.max(z, axis=-1, keepdims=True)
            p = jnp.exp(z - m)
            den = jnp.sum(p, axis=-1, keepdims=True)
            out = _dot(p.astype(BF16), vb) / den
            lse = m + jnp.log(den)
            og[gi, q_rows, :] = jnp.where(lane < HEAD_DIM, out[:span], out[span:])
            lg[gi, q_rows, :] = jnp.where(lane < HEAD_DIM, lse[:span], lse[span:])
            return 0

        lax.fori_loop(0, dil * nblk, body, 0, unroll=4)

    lse = lg[...]
    top = jnp.max(lse, axis=0)
    wts = jnp.exp(lse - top[None])
    o = jnp.sum(wts * og[...], axis=0) / jnp.sum(wts, axis=0)
    o_ref[0] = _pair_rms_norm(o, g_ref[...], lane).astype(o_ref.dtype)


def _dil_attention(proj, cos_t, sin_t, gain):
    bsz, seq, _ = proj.shape
    pairs = W_DIL // LANES
    pad = DIL_SPAN * max(DILATIONS)
    assert all(seq % (d * DIL_SPAN) == 0 for d in DILATIONS)
    return pl.pallas_call(
        functools.partial(_dil_kernel, seq=seq, pad=pad),
        grid=(bsz, pairs),
        in_specs=[
            pl.BlockSpec((1, seq, LANES), lambda b, p: (b, 0, p)),
            pl.BlockSpec((1, seq, LANES), lambda b, p: (b, 0, pairs + p)),
            pl.BlockSpec((1, seq, LANES), lambda b, p: (b, 0, 2 * pairs + p)),
            pl.BlockSpec((1, seq, LANES), lambda b, p: (b, 0, 0)),
            pl.BlockSpec((1, seq, LANES), lambda b, p: (b, 0, 0)),
            pl.BlockSpec((1, LANES), lambda b, p: (0, p)),
        ],
        out_specs=pl.BlockSpec((1, seq, LANES), lambda b, p: (b, 0, p)),
        out_shape=jax.ShapeDtypeStruct((bsz, seq, W_DIL), BF16),
        scratch_shapes=[
            pltpu.VMEM((seq, LANES), F32),
            pltpu.VMEM((pad + seq, LANES), F32),
            pltpu.VMEM((pad + seq, LANES), F32),
            pltpu.VMEM((len(DILATIONS), seq, LANES), F32),
            pltpu.VMEM((len(DILATIONS), seq, LANES), F32),
        ],
        compiler_params=_params(("arbitrary", "arbitrary")),
        name="dilated_attention",
    )(proj, proj, proj, cos_t, sin_t, gain.reshape(1, W_DIL))


def _oproj_kernel(osb_ref, ofx_ref, odl_ref, w1_ref, w2_ref, w3_ref, x_ref, gate_ref, sc_ref, sh_ref,
                  lng_ref, lnb_ref, wr_ref, rb_ref, x1_ref, h2_ref, h2s_ref, idx_ref, gw_ref, rank_ref,
                  cnt_ref, seen_ref):
    @pl.when((pl.program_id(0) == 0) & (pl.program_id(1) == 0))
    def _():
        seen_ref[...] = jnp.zeros_like(seen_ref)

    mix = (_dot(osb_ref[0], w1_ref[...]) + _dot(ofx_ref[0], w2_ref[...])
           + _dot(odl_ref[0], w3_ref[...]))
    x1 = _layer_norm(DEEPNORM_ALPHA * x_ref[0] + gate_ref[0] * mix, lng_ref[...], lnb_ref[...])
    x1_ref[0] = x1
    h2 = x1 * (1.0 + sc_ref[0]) + sh_ref[0]
    h2_ref[0] = h2
    _store_row_slabs(h2s_ref, h2)

    scores = jax.nn.sigmoid(_dot_f32(h2, wr_ref[...]))
    tm = scores.shape[0]
    lane = lax.broadcasted_iota(I32, (1, LANES), 1)
    lane_f = lane.astype(F32)
    sel = jnp.where(lane < N_EXPERTS, scores + rb_ref[...], NEG_INF)
    idx_acc = jnp.zeros((tm, LANES), I32)
    gate_acc = jnp.zeros((tm, LANES), F32)
    total = jnp.zeros((tm, 1), F32)
    onehot = jnp.zeros((tm, LANES), F32)
    picks = []
    for k in range(TOP_K):
        best = jnp.max(sel, axis=-1, keepdims=True)
        pick = jnp.min(jnp.where(sel == best, lane_f, float(LANES)), axis=-1, keepdims=True)
        chosen = lane_f == pick
        s = jnp.sum(jnp.where(chosen, scores, 0.0), axis=-1, keepdims=True)
        sel = jnp.where(chosen, NEG_INF, sel)
        idx_acc = jnp.where(lane == k, pick.astype(I32), idx_acc)
        gate_acc = jnp.where(lane == k, s, gate_acc)
        onehot = jnp.where(chosen, 1.0, onehot)
        total = total + s
        picks.append(pick)
    idx_ref[0] = idx_acc[:, :8]
    gw_ref[0] = (gate_acc / total * ROUTED_SCALE)[:, :8]

    r = lax.broadcasted_iota(I32, (tm, tm), 0)
    c = lax.broadcasted_iota(I32, (tm, tm), 1)
    earlier = jnp.where(c < r, 1.0, 0.0).astype(BF16)
    rank = _dot(earlier, onehot.astype(BF16)) + seen_ref[...]
    rank_acc = jnp.zeros((tm, LANES), F32)
    for k in range(TOP_K):
        rk = jnp.sum(jnp.where(lane_f == picks[k], rank, 0.0), axis=-1, keepdims=True)
        rank_acc = jnp.where(lane == k, rk, rank_acc)
    rank_ref[0] = rank_acc[:, :8].astype(I32)
    seen_ref[...] = seen_ref[...] + jnp.sum(onehot, axis=0, keepdims=True)
    cnt_ref[...] = seen_ref[...]


def _output_projection(o_sb, o_fx, o_dl, w_out, x, gate1, scale2, shift2, ln_g, ln_b, w_router,
                       router_bias, tm=256):
    bsz, seq, d = x.shape
    pitch = _slab_pitch(d)
    w1 = w_out[:W_SB].astype(BF16)
    w2 = w_out[W_SB:W_SB + W_FOX].astype(BF16)
    w3 = w_out[W_SB + W_FOX:].astype(BF16)
    wr = jnp.zeros((d, LANES), F32).at[:, :N_EXPERTS].set(w_router.astype(F32))
    rb = jnp.zeros((1, LANES), F32).at[0, :N_EXPERTS].set(router_bias.astype(F32))
    tile = lambda w: pl.BlockSpec((1, tm, w), lambda b, i: (b, i, 0))
    whole = lambda a: pl.BlockSpec(a.shape, lambda b, i: (0,) * a.ndim)
    per_batch = pl.BlockSpec((1, 1, d), lambda b, i: (b, 0, 0))
    return pl.pallas_call(
        _oproj_kernel,
        grid=(bsz, seq // tm),
        in_specs=[tile(W_SB), tile(W_FOX), tile(W_DIL), whole(w1), whole(w2), whole(w3), tile(d),
                  per_batch, per_batch, per_batch,
                  pl.BlockSpec((1, d), lambda b, i: (0, 0)), pl.BlockSpec((1, d), lambda b, i: (0, 0)),
                  whole(wr), whole(rb)],
        out_specs=[tile(d), tile(d),
                   pl.BlockSpec((tm * pitch, LANES), lambda b, i: (b * (seq // tm) + i, 0)),
                   tile(8), tile(8), tile(8),
                   pl.BlockSpec((1, LANES), lambda b, i: (0, 0))],
        out_shape=[jax.ShapeDtypeStruct((bsz, seq, d), F32), jax.ShapeDtypeStruct((bsz, seq, d), F32),
                   jax.ShapeDtypeStruct((bsz * seq * pitch, LANES), F32),
                   jax.ShapeDtypeStruct((bsz, seq, 8), I32), jax.ShapeDtypeStruct((bsz, seq, 8), F32),
                   jax.ShapeDtypeStruct((bsz, seq, 8), I32), jax.ShapeDtypeStruct((1, LANES), F32)],
        scratch_shapes=[pltpu.VMEM((1, LANES), F32)],
        compiler_params=_params(("arbitrary", "arbitrary")),
        name="output_projection_router",
    )(o_sb, o_fx, o_dl, w1, w2, w3, x, gate1, scale2, shift2, ln_g.reshape(1, d), ln_b.reshape(1, d),
      wr, rb)


def _dispatch_plan(idx, rank, counts, rows_per_block):
    n_tok = idx.shape[0]
    n_assign = n_tok * TOP_K
    padded = (counts + rows_per_block - 1) // rows_per_block * rows_per_block
    pad_end = jnp.cumsum(padded)
    pad_start = pad_end - padded
    dest_of = (jnp.take(pad_start, idx) + rank).astype(I32)
    n_blocks = -(-(n_assign + N_EXPERTS * (rows_per_block - 1)) // rows_per_block)
    rows = n_blocks * rows_per_block
    tok = jnp.broadcast_to(jnp.arange(n_tok, dtype=I32)[:, None], (n_tok, TOP_K))
    row_tok = jnp.zeros((rows,), I32).at[dest_of.reshape(-1)].set(tok.reshape(-1))
    block_first_row = jnp.arange(n_blocks, dtype=I32) * rows_per_block
    block_e = jnp.sum((pad_end[None, :] <= block_first_row[:, None]).astype(I32), axis=1)
    block_e = jnp.minimum(block_e, N_EXPERTS - 1).astype(I32)
    n_used = (pad_end[-1] // rows_per_block).astype(I32).reshape(1)
    return row_tok, block_e, n_used, dest_of, n_blocks


def _row_copy(src_hbm, first, dst, slot, r, sem, d):
    slabs = d // LANES
    return pltpu.make_async_copy(src_hbm.at[pl.ds(first, slabs), :],
                                 dst.at[slot, pl.ds(r * _slab_pitch(d), slabs), :], sem.at[slot])


def _expert_kernel(be_ref, nu_ref, tok_cur, tok_next, h_hbm, wg_ref, wu_ref, wd_ref, y_ref,
                   xbuf, sem, wg_bf, wu_bf, wd_bf, *, rows):
    d = wg_bf.shape[0]
    i = pl.program_id(0)
    slot = lax.rem(i, 2)
    n_used = nu_ref[0]

    def start_gather(tok_ref, s):
        def body(r, _):
            _row_copy(h_hbm, tok_ref[0, 0, r], xbuf, s, r, sem, d).start()
            return 0
        lax.fori_loop(0, rows, body, 0, unroll=8)

    @pl.when(i == 0)
    def _():
        start_gather(tok_cur, 0)

    @pl.when(i + 1 < n_used)
    def _():
        start_gather(tok_next, 1 - slot)

    new_expert = (i == 0) | (be_ref[i] != be_ref[jnp.maximum(i - 1, 0)])

    @pl.when(new_expert & (i < n_used))
    def _():
        wg_bf[...] = wg_ref[0, 0].astype(BF16)
        wu_bf[...] = wu_ref[0, 0].astype(BF16)
        wd_bf[...] = wd_ref[0, 0].astype(BF16)

    @pl.when(i < n_used)
    def _():
        def wait(r, _):
            _row_copy(h_hbm, 0, xbuf, slot, r, sem, d).wait()
            return 0
        lax.fori_loop(0, rows, wait, 0, unroll=8)
        xb = _load_row_slabs(xbuf.at[slot], rows, d, BF16)
        hidden = _silu(_dot(xb, wg_bf[...])) * _dot(xb, wu_bf[...])
        _store_row_slabs(y_ref, _dot(hidden.astype(BF16), wd_bf[...]))

    @pl.when(i >= n_used)
    def _():
        y_ref[...] = jnp.zeros_like(y_ref)


def _grouped_experts(h2, row_tok, block_e, n_used, n_blocks, layer, w_gate, w_up, w_down,
                     rows=MOE_ROWS):
    d, hid = w_gate.shape[-2:]
    pitch = _slab_pitch(d)
    tok3 = (row_tok * pitch).reshape(n_blocks, 1, rows)
    smem_rows = lambda shift: pl.BlockSpec(
        (1, 1, rows), lambda i, be, nu: (jnp.minimum(i + shift, n_blocks - 1), 0, 0),
        memory_space=pltpu.SMEM)
    grid_spec = pltpu.PrefetchScalarGridSpec(
        num_scalar_prefetch=2,
        grid=(n_blocks,),
        in_specs=[
            smem_rows(0), smem_rows(1),
            pl.BlockSpec(memory_space=pl.ANY),
            pl.BlockSpec((1, 1, d, hid), lambda i, be, nu: (layer, be[i], 0, 0)),
            pl.BlockSpec((1, 1, d, hid), lambda i, be, nu: (layer, be[i], 0, 0)),
            pl.BlockSpec((1, 1, hid, d), lambda i, be, nu: (layer, be[i], 0, 0)),
        ],
        out_specs=pl.BlockSpec((rows * pitch, LANES), lambda i, be, nu: (i, 0)),
        scratch_shapes=[
            pltpu.VMEM((2, rows * pitch, LANES), F32),
            pltpu.SemaphoreType.DMA((2,)),
            pltpu.VMEM((d, hid), BF16),
            pltpu.VMEM((d, hid), BF16),
            pltpu.VMEM((hid, d), BF16),
        ],
    )
    return pl.pallas_call(
        functools.partial(_expert_kernel, rows=rows),
        grid_spec=grid_spec,
        out_shape=jax.ShapeDtypeStruct((n_blocks * rows * pitch, LANES), F32),
        compiler_params=_params(("arbitrary",)),
        name="grouped_experts",
    )(block_e, n_used, tok3, tok3, h2, w_gate, w_up, w_down)


def _combine_kernel(dest_cur, dest_next, y_hbm, gw_ref, h2_ref, x1_ref, gate_ref, lng_ref, lnb_ref,
                    wg_ref, wu_ref, wd_ref, o_ref, buf, gws, routed, sem, *, tm):
    d = o_ref.shape[1]
    slabs = d // LANES
    pitch = _slab_pitch(d)
    i = pl.program_id(0)
    slot = lax.rem(i, 2)
    n_rows = TOP_K * tm

    def start_gather(dest_ref, s):
        def body(r, _):
            _row_copy(y_hbm, dest_ref[0, 0, r], buf, s, r, sem, d).start()
            return 0
        lax.fori_loop(0, n_rows, body, 0, unroll=8)

    @pl.when(i == 0)
    def _():
        for s in range(2):
            buf[s, pl.ds(slabs, n_rows, stride=pitch), :] = jnp.zeros((n_rows, LANES), F32)
        start_gather(dest_cur, 0)

    @pl.when(i + 1 < pl.num_programs(0))
    def _():
        start_gather(dest_next, 1 - slot)

    hb = h2_ref[...].astype(BF16)
    hidden = _silu(_dot(hb, wg_ref[...])) * _dot(hb, wu_ref[...])
    ffn = _dot(hidden.astype(BF16), wd_ref[...])

    def wait(r, _):
        _row_copy(y_hbm, 0, buf, slot, r, sem, d).wait()
        return 0
    lax.fori_loop(0, n_rows, wait, 0, unroll=8)
    gw = gw_ref[...]
    for c in range(pitch):
        gws[pl.ds(c, tm, stride=pitch), :] = gw
    gw = gws[...]
    span = tm * pitch
    acc = gw[:, 0:1] * buf[slot, 0:span, :]
    for k in range(1, TOP_K):
        acc = acc + gw[:, k:k + 1] * buf[slot, k * span:(k + 1) * span, :]
    routed[...] = acc
    ffn = ffn + _load_row_slabs(routed, tm, d, F32)
    o_ref[...] = _layer_norm(DEEPNORM_ALPHA * x1_ref[...] + gate_ref[0] * ffn, lng_ref[...], lnb_ref[...])


def _combine(y, dest_of, gates, h2, x1, gate2, ln_g, ln_b, ws_gate, ws_up, ws_down, seq,
             tm=COMBINE_TOKENS):
    n_tok, d = h2.shape
    n_tiles = n_tok // tm
    hid = ws_gate.shape[-1]
    pitch = _slab_pitch(d)
    dest3 = (dest_of * pitch).reshape(n_tiles, tm, TOP_K).transpose(0, 2, 1).reshape(n_tiles, 1, TOP_K * tm)
    smem_rows = lambda shift: pl.BlockSpec(
        (1, 1, TOP_K * tm), lambda i: (jnp.minimum(i + shift, n_tiles - 1), 0, 0),
        memory_space=pltpu.SMEM)
    tile = pl.BlockSpec((tm, d), lambda i: (i, 0))
    row = pl.BlockSpec((1, d), lambda i: (0, 0))
    return pl.pallas_call(
        functools.partial(_combine_kernel, tm=tm),
        grid=(n_tiles,),
        in_specs=[
            smem_rows(0), smem_rows(1),
            pl.BlockSpec(memory_space=pl.ANY),
            pl.BlockSpec((tm, 8), lambda i: (i, 0)),
            tile, tile,
            pl.BlockSpec((1, 1, d), lambda i: (i * tm // seq, 0, 0)),
            row, row,
            pl.BlockSpec((d, hid), lambda i: (0, 0)),
            pl.BlockSpec((d, hid), lambda i: (0, 0)),
            pl.BlockSpec((hid, d), lambda i: (0, 0)),
        ],
        out_specs=tile,
        out_shape=jax.ShapeDtypeStruct((n_tok, d), F32),
        scratch_shapes=[pltpu.VMEM((2, TOP_K * tm * pitch, LANES), F32),
                        pltpu.VMEM((tm * pitch, 8), F32),
                        pltpu.VMEM((tm * pitch, LANES), F32), pltpu.SemaphoreType.DMA((2,))],
        compiler_params=_params(("arbitrary",)),
        name="combine_shared_layernorm",
    )(dest3, dest3, y, gates, h2, x1, gate2, ln_g.reshape(1, d), ln_b.reshape(1, d),
      ws_gate.astype(BF16), ws_up.astype(BF16), ws_down.astype(BF16))


def kernel(x, c, positions, w_ada, b_ada, w_in, b_forget, head_norm_g, w_out, ln1_g, ln1_b, w_router,
           router_bias, w_exp_gate, w_exp_up, w_exp_down, w_sh_gate, w_sh_up, w_sh_down, ln2_g, ln2_b):
    bsz, seq, d = x.shape
    n_tok = bsz * seq
    mod = _ada_modulation(c, w_ada, b_ada)
    cos_t, sin_t = _rope_tables(positions)
    n_a = 3 * (W_SB + W_FOX)
    n_qkv = n_a + 3 * W_DIL
    for l in range(DEPTH):
        shift1, scale1, gate1, shift2, scale2, gate2 = (
            mod[l, :, k * d:(k + 1) * d].reshape(bsz, 1, d) for k in range(6))
        w_a = w_in[l, :, :n_a].astype(BF16)
        w_f = jnp.zeros((d, LANES), F32).at[:, :N_HEADS_FOX].set(w_in[l, :, n_qkv:])
        w_b = jnp.concatenate([w_in[l, :, n_a:n_qkv], w_f], axis=1).astype(BF16)
        proj_a = _modulated_matmul(x, scale1, shift1, w_a, BF16, tm=512, tn=n_a // 3)
        proj_b = _modulated_matmul(x, scale1, shift1, w_b, F32, tm=512, tn=w_b.shape[1])
        g = head_norm_g[l]
        o_sb = _sb_attention(proj_a, g[:W_SB])
        cum_col, cum_row = _fox_prefix(proj_b, 3 * W_DIL // LANES, b_forget[l])
        cum_row = cum_row[:, :N_HEADS_FOX].reshape(bsz, N_HEADS_FOX // 2, 2, seq)
        o_fx = _fox_attention(proj_a, 3 * W_SB // LANES, cum_col, cum_row, g[W_SB:W_SB + W_FOX])
        o_dl = _dil_attention(proj_b, cos_t, sin_t, g[W_SB + W_FOX:])
        x1, h2, h2_slabs, idx, gates, rank, counts = _output_projection(
            o_sb, o_fx, o_dl, w_out[l], x, gate1, scale2, shift2, ln1_g[l], ln1_b[l], w_router[l],
            router_bias[l])
        h2 = h2.reshape(n_tok, d)
        row_tok, block_e, n_used, dest_of, n_blocks = _dispatch_plan(
            idx.reshape(n_tok, 8)[:, :TOP_K], rank.reshape(n_tok, 8)[:, :TOP_K],
            counts[0, :N_EXPERTS].astype(I32), MOE_ROWS)
        y = _grouped_experts(h2_slabs, row_tok, block_e, n_used, n_blocks, l,
                             w_exp_gate, w_exp_up, w_exp_down)
        x = _combine(y, dest_of, gates.reshape(n_tok, 8), h2, x1.reshape(n_tok, d), gate2, ln2_g[l],
                     ln2_b[l], w_sh_gate[l], w_sh_up[l], w_sh_down[l], seq).reshape(bsz, seq, d)
    return x
```

```python
import functools

import jax
import jax.numpy as jnp
from jax import lax
from jax.experimental import pallas as pl
from jax.experimental.pallas import tpu as pltpu

F32 = jnp.float32
BF16 = jnp.bfloat16
I32 = jnp.int32
U32 = jnp.uint32

D_MODEL = 2048
DEPTH = 2
HEAD_DIM = 64
N_HEADS_SB = 8
N_HEADS_FOX = 12
N_HEADS_DIL = 12
W_SB = N_HEADS_SB * HEAD_DIM
W_FOX = N_HEADS_FOX * HEAD_DIM
W_DIL = N_HEADS_DIL * HEAD_DIM
DILATIONS = (1, 4, 16)
DIL_SPAN = 128
ROPE_THETA = 10000.0
N_EXPERTS = 64
TOP_K = 6
ROUTED_SCALE = 2.5
DEEPNORM_ALPHA = (2 * DEPTH) ** 0.25
LN_EPS = 1e-5
RMS_EPS = 1e-6
QK_SCALE = HEAD_DIM ** -0.5
NEG_INF = float("-inf")

LANES = 128
ATTN_QUERY_BLOCK = 512
ATTN_KEY_BLOCK = 256
MOE_ROWS = 256
COMBINE_TOKENS = 128
VMEM_LIMIT = 56 * 1024 * 1024


def _params(semantics, vmem=VMEM_LIMIT):
    return pltpu.CompilerParams(dimension_semantics=semantics, vmem_limit_bytes=vmem)


def _dot(a, b):
    return jnp.dot(a, b, preferred_element_type=F32)


def _dot_nt(a, b):
    return lax.dot_general(a, b, (((1,), (1,)), ((), ())), preferred_element_type=F32)


def _split2(a):
    hi = a.astype(BF16)
    lo = (a - hi.astype(F32)).astype(BF16)
    return hi, lo


def _split3(a):
    hi = a.astype(BF16)
    r1 = a - hi.astype(F32)
    mid = r1.astype(BF16)
    lo = (r1 - mid.astype(F32)).astype(BF16)
    return hi, mid, lo


def _dot_f32(a, b):
    ah, al = _split2(a)
    bh, bl = _split2(b)
    return _dot(ah, bh) + _dot(ah, bl) + _dot(al, bh)


def _softplus(z):
    return jnp.maximum(z, 0.0) + jnp.log1p(jnp.exp(-jnp.abs(z)))


def _silu(z):
    return z * jax.nn.sigmoid(z)


def _layer_norm(r, g, b):
    mu = jnp.mean(r, axis=-1, keepdims=True)
    d = r - mu
    var = jnp.mean(d * d, axis=-1, keepdims=True)
    return d * lax.rsqrt(var + LN_EPS) * g + b


def _slab_pitch(width):
    return width // LANES + 1


def _store_row_slabs(dst, value):
    n, width = value.shape
    slabs = width // LANES
    pitch = _slab_pitch(width)
    for c in range(slabs):
        dst[pl.ds(c, n, stride=pitch), :] = value[:, c * LANES:(c + 1) * LANES]
    dst[pl.ds(slabs, n, stride=pitch), :] = jnp.zeros((n, LANES), dst.dtype)


def _load_row_slabs(src, n, width):
    pitch = _slab_pitch(width)
    return jnp.concatenate([src[pl.ds(c, n, stride=pitch), :] for c in range(width // LANES)], axis=1)


def _pack_bf16_pairs(v):
    half = v.shape[1] // 2
    lo = lax.bitcast_convert_type(v[:, :half].astype(BF16).astype(F32), U32)
    hi = lax.bitcast_convert_type(v[:, half:].astype(BF16).astype(F32), U32)
    return hi | (lo >> 16)


def _unpack_bf16_pairs(w):
    lo = lax.bitcast_convert_type(w << 16, F32)
    hi = lax.bitcast_convert_type(w & jnp.uint32(0xFFFF0000), F32)
    return lo, hi


def _pair_rms_norm(o, g, lane):
    left = lane < HEAD_DIM
    sq = o * o
    s0 = jnp.sum(jnp.where(left, sq, 0.0), axis=-1, keepdims=True)
    s1 = jnp.sum(jnp.where(left, 0.0, sq), axis=-1, keepdims=True)
    ms = jnp.where(left, s0, s1) * (1.0 / HEAD_DIM)
    return o * lax.rsqrt(ms + RMS_EPS) * g


def _ada_kernel(c_ref, w_ref, b_ref, o_ref):
    o_ref[0] = _dot_f32(_silu(c_ref[...]), w_ref[0]) + b_ref[0]


def _ada_modulation(c, w_ada, b_ada, tn=1024):
    depth, d, n = w_ada.shape
    bsz = c.shape[0]
    return pl.pallas_call(
        _ada_kernel,
        grid=(depth, n // tn),
        in_specs=[
            pl.BlockSpec((bsz, d), lambda l, j: (0, 0)),
            pl.BlockSpec((1, d, tn), lambda l, j: (l, 0, j)),
            pl.BlockSpec((1, 1, tn), lambda l, j: (l, 0, j)),
        ],
        out_specs=pl.BlockSpec((1, bsz, tn), lambda l, j: (l, 0, j)),
        out_shape=jax.ShapeDtypeStruct((depth, bsz, n), F32),
        compiler_params=_params(("arbitrary", "arbitrary")),
        name="ada_modulation",
    )(c, w_ada, b_ada.reshape(depth, 1, n))


def _rope_kernel(pos_ref, invf_ref, sign_ref, cos_ref, sin_ref):
    ang = pos_ref[0].astype(F32) * invf_ref[...]
    cos_ref[0] = jnp.cos(ang)
    sin_ref[0] = jnp.sin(ang) * sign_ref[...]


def _rope_tables(positions, ts=512):
    bsz, seq = positions.shape
    half = HEAD_DIM // 2
    inv_freq = ROPE_THETA ** (-jnp.arange(half, dtype=F32) / half)
    inv_lane = jnp.tile(inv_freq, LANES // half).reshape(1, LANES)
    sign = jnp.tile(jnp.concatenate([-jnp.ones((half,), F32), jnp.ones((half,), F32)]),
                    LANES // HEAD_DIM).reshape(1, LANES)
    tab = jax.ShapeDtypeStruct((bsz, seq, LANES), F32)
    return pl.pallas_call(
        _rope_kernel,
        grid=(bsz, seq // ts),
        in_specs=[
            pl.BlockSpec((1, ts, 1), lambda b, i: (b, i, 0)),
            pl.BlockSpec((1, LANES), lambda b, i: (0, 0)),
            pl.BlockSpec((1, LANES), lambda b, i: (0, 0)),
        ],
        out_specs=[pl.BlockSpec((1, ts, LANES), lambda b, i: (b, i, 0))] * 2,
        out_shape=[tab, tab],
        compiler_params=_params(("arbitrary", "arbitrary")),
        name="rope_tables",
    )(positions.reshape(bsz, seq, 1), inv_lane, sign)


def _modmm_kernel(x_ref, sc_ref, sh_ref, w_ref, o_ref):
    h = x_ref[0] * (1.0 + sc_ref[0]) + sh_ref[0]
    o_ref[0] = _dot(h.astype(BF16), w_ref[...]).astype(o_ref.dtype)


def _modulated_matmul(x, scale, shift, w, out_dtype, tm, tn):
    bsz, seq, d = x.shape
    n = w.shape[1]
    return pl.pallas_call(
        _modmm_kernel,
        grid=(n // tn, bsz, seq // tm),
        in_specs=[
            pl.BlockSpec((1, tm, d), lambda j, b, i: (b, i, 0)),
            pl.BlockSpec((1, 1, d), lambda j, b, i: (b, 0, 0)),
            pl.BlockSpec((1, 1, d), lambda j, b, i: (b, 0, 0)),
            pl.BlockSpec((d, tn), lambda j, b, i: (0, j)),
        ],
        out_specs=pl.BlockSpec((1, tm, tn), lambda j, b, i: (b, i, j)),
        out_shape=jax.ShapeDtypeStruct((bsz, seq, n), out_dtype),
        compiler_params=_params(("arbitrary", "arbitrary", "arbitrary")),
        name="modulated_projection",
    )(x, scale, shift, w)


def _stack_pair_queries(q, lane):
    zero = jnp.zeros_like(q)
    return jnp.concatenate([jnp.where(lane < HEAD_DIM, q, zero), jnp.where(lane < HEAD_DIM, zero, q)], axis=0)


def _sb_kernel(q_ref, k_ref, v_ref, g_ref, o_ref, *, tq, tk):
    qi = pl.program_id(2)
    sub = tq // tk
    lane = lax.broadcasted_iota(I32, (1, LANES), 1)
    row = lax.broadcasted_iota(I32, (2 * tq, tk), 0) & (tq - 1)
    col = lax.broadcasted_iota(I32, (2 * tq, tk), 1)
    r2 = lax.broadcasted_iota(I32, (tk, tk), 0)
    c2 = lax.broadcasted_iota(I32, (tk, tk), 1)
    later_keys = jnp.where(r2 > c2, 1.0, 0.0).astype(BF16)
    q2 = _stack_pair_queries(q_ref[0] * QK_SCALE, lane)

    def block(kj, carry, diag):
        o, tail = carry
        start = pl.multiple_of(kj * tk, tk)
        kb = k_ref[0, pl.ds(start, tk), :]
        vb = v_ref[0, pl.ds(start, tk), :]
        z = _dot_nt(q2, kb)
        log_keep = -jnp.log(1.0 + jnp.exp(-jnp.abs(z))) - jnp.maximum(z, 0.0)
        if diag is not None:
            past = col + diag * tk < row
            log_keep = jnp.where(past, log_keep, 0.0)
        hi, lo = _split2(log_keep)
        sums = _dot(jnp.concatenate([hi, lo], axis=0), later_keys)
        later = sums[:2 * tq] + sums[2 * tq:] + tail
        w = jnp.exp(z + log_keep + later)
        if diag is not None:
            w = jnp.where(past, w, 0.0)
        o = o + _dot(w.astype(BF16), vb)
        tail = tail + jnp.sum(log_keep, axis=-1, keepdims=True)
        return o, tail

    carry = (jnp.zeros((2 * tq, LANES), F32), jnp.zeros((2 * tq, 1), F32))
    for diag in reversed(range(sub)):
        carry = block(qi * sub + diag, carry, diag)
    carry = lax.fori_loop(0, qi * sub, lambda it, c: block(qi * sub - 1 - it, c, None), carry)
    o = jnp.where(lane < HEAD_DIM, carry[0][:tq], carry[0][tq:])
    o_ref[0] = _pair_rms_norm(o, g_ref[...], lane).astype(o_ref.dtype)


def _sb_attention(proj, gain, blk=ATTN_QUERY_BLOCK):
    bsz, seq, _ = proj.shape
    pairs = W_SB // LANES
    return pl.pallas_call(
        functools.partial(_sb_kernel, tq=blk, tk=ATTN_KEY_BLOCK),
        grid=(bsz, pairs, seq // blk),
        in_specs=[
            pl.BlockSpec((1, blk, LANES), lambda b, p, i: (b, i, p)),
            pl.BlockSpec((1, seq, LANES), lambda b, p, i: (b, 0, pairs + p)),
            pl.BlockSpec((1, seq, LANES), lambda b, p, i: (b, 0, 2 * pairs + p)),
            pl.BlockSpec((1, LANES), lambda b, p, i: (0, p)),
        ],
        out_specs=pl.BlockSpec((1, blk, LANES), lambda b, p, i: (b, i, p)),
        out_shape=jax.ShapeDtypeStruct((bsz, seq, W_SB), BF16),
        compiler_params=_params(("arbitrary", "arbitrary", "arbitrary")),
        name="stick_breaking_attention",
    )(proj, proj, proj, gain.reshape(1, W_SB))


def _fox_prep_kernel(f_ref, b_ref, col_ref, row_ref, *, blk):
    seq = f_ref.shape[1]
    x = f_ref[0] + b_ref[...]
    log_f = -_softplus(-x)
    r = lax.broadcasted_iota(I32, (blk, blk), 0)
    c = lax.broadcasted_iota(I32, (blk, blk), 1)
    upto = jnp.where(c <= r, 1.0, 0.0).astype(BF16)
    carry = jnp.zeros((1, LANES), F32)
    parts = []
    for i in range(seq // blk):
        hi, mid, lo = _split3(log_f[i * blk:(i + 1) * blk])
        cs = _dot(upto, hi) + _dot(upto, mid) + _dot(upto, lo) + carry
        parts.append(cs)
        carry = cs[blk - 1:blk, :]
    cum = jnp.concatenate(parts, axis=0)
    col_ref[0] = cum
    row_ref[0] = cum.T


def _fox_prefix(f_logits, f_block, b_forget):
    bsz, seq, _ = f_logits.shape
    bias = jnp.zeros((1, LANES), F32).at[0, :N_HEADS_FOX].set(b_forget)
    return pl.pallas_call(
        functools.partial(_fox_prep_kernel, blk=256),
        grid=(bsz,),
        in_specs=[
            pl.BlockSpec((1, seq, LANES), lambda b: (b, 0, f_block)),
            pl.BlockSpec((1, LANES), lambda b: (0, 0)),
        ],
        out_specs=[
            pl.BlockSpec((1, seq, LANES), lambda b: (b, 0, 0)),
            pl.BlockSpec((1, LANES, seq), lambda b: (b, 0, 0)),
        ],
        out_shape=[jax.ShapeDtypeStruct((bsz, seq, LANES), F32),
                   jax.ShapeDtypeStruct((bsz, LANES, seq), F32)],
        compiler_params=_params(("arbitrary",)),
        name="forget_prefix",
    )(f_logits, bias)


def _fox_kernel(q_ref, k_ref, v_ref, cq_ref, ck_ref, g_ref, o_ref, *, tq, tk):
    pair = pl.program_id(1)
    qi = pl.program_id(2)
    sub = tq // tk
    lane = lax.broadcasted_iota(I32, (1, LANES), 1)
    row = lax.broadcasted_iota(I32, (2 * tq, tk), 0) & (tq - 1)
    col = lax.broadcasted_iota(I32, (2 * tq, tk), 1)
    q2 = _stack_pair_queries(q_ref[0] * QK_SCALE, lane)
    cum_q_all = cq_ref[0]
    cum_q = [jnp.sum(jnp.where(lane == 2 * pair + h, cum_q_all, 0.0), axis=-1, keepdims=True)
             for h in range(2)]

    def block(kj, carry, diag):
        m, l, o = carry
        start = pl.multiple_of(kj * tk, tk)
        kb = k_ref[0, pl.ds(start, tk), :]
        vb = v_ref[0, pl.ds(start, tk), :]
        qk = _dot_nt(q2, kb)
        z = jnp.concatenate(
            [qk[h * tq:(h + 1) * tq] + (cum_q[h] - ck_ref[0, 0, h:h + 1, pl.ds(start, tk)])
             for h in range(2)], axis=0)
        if diag is not None:
            z = jnp.where(col + diag * tk <= row, z, NEG_INF)
        m_new = jnp.maximum(m, jnp.max(z, axis=-1, keepdims=True))
        p = jnp.exp(z - m_new)
        a = jnp.exp(m - m_new)
        l = a * l + jnp.sum(p, axis=-1, keepdims=True)
        o = a * o + _dot(p.astype(BF16), vb)
        return m_new, l, o

    carry = (jnp.full((2 * tq, 1), NEG_INF, F32), jnp.zeros((2 * tq, 1), F32),
             jnp.zeros((2 * tq, LANES), F32))
    for diag in range(sub):
        carry = block(qi * sub + diag, carry, diag)
    carry = lax.fori_loop(0, qi * sub, lambda kj, c: block(kj, c, None), carry)
    o = carry[2] / carry[1]
    o = jnp.where(lane < HEAD_DIM, o[:tq], o[tq:])
    o_ref[0] = _pair_rms_norm(o, g_ref[...], lane).astype(o_ref.dtype)


def _fox_attention(proj, col_base, cum_col, cum_row, gain, blk=ATTN_QUERY_BLOCK):
    bsz, seq, _ = proj.shape
    pairs = W_FOX // LANES
    return pl.pallas_call(
        functools.partial(_fox_kernel, tq=blk, tk=ATTN_KEY_BLOCK),
        grid=(bsz, pairs, seq // blk),
        in_specs=[
            pl.BlockSpec((1, blk, LANES), lambda b, p, i: (b, i, col_base + p)),
            pl.BlockSpec((1, seq, LANES), lambda b, p, i: (b, 0, col_base + pairs + p)),
            pl.BlockSpec((1, seq, LANES), lambda b, p, i: (b, 0, col_base + 2 * pairs + p)),
            pl.BlockSpec((1, blk, LANES), lambda b, p, i: (b, i, 0)),
            pl.BlockSpec((1, 1, 2, seq), lambda b, p, i: (b, p, 0, 0)),
            pl.BlockSpec((1, LANES), lambda b, p, i: (0, p)),
        ],
        out_specs=pl.BlockSpec((1, blk, LANES), lambda b, p, i: (b, i, p)),
        out_shape=jax.ShapeDtypeStruct((bsz, seq, W_FOX), BF16),
        compiler_params=_params(("arbitrary", "arbitrary", "arbitrary")),
        name="forgetting_attention",
    )(proj, proj, proj, cum_col, cum_row, gain.reshape(1, W_FOX))


def _dil_kernel(q_ref, k_ref, v_ref, cos_ref, sin_ref, g_ref, o_ref,
                qs, ks, vs, og, lg, *, seq, pad):
    lane = lax.broadcasted_iota(I32, (1, LANES), 1)
    first_half = (lane & (HEAD_DIM - 1)) < HEAD_DIM // 2
    cos = cos_ref[0]
    sin = sin_ref[0]

    def rope(t):
        partner = jnp.where(first_half, pltpu.roll(t, LANES - HEAD_DIM // 2, 1),
                            pltpu.roll(t, HEAD_DIM // 2, 1))
        return t * cos + partner * sin

    qs[...] = rope(q_ref[0]) * QK_SCALE
    ks[0:pad, :] = jnp.zeros((pad, LANES), F32)
    vs[0:pad, :] = jnp.zeros((pad, LANES), F32)
    ks[pad:pad + seq, :] = rope(k_ref[0])
    vs[pad:pad + seq, :] = v_ref[0]

    span = DIL_SPAN
    qpos = lax.broadcasted_iota(I32, (2 * span, 2 * span), 0) & (span - 1)
    kpos = lax.broadcasted_iota(I32, (2 * span, 2 * span), 1)
    band = (kpos >= qpos) & (kpos <= qpos + span)

    for gi, dil in enumerate(DILATIONS):
        nblk = seq // dil // span

        def body(idx, _, gi=gi, dil=dil):
            r = lax.rem(idx, dil)
            n = lax.div(idx, dil)
            q0 = r + n * (span * dil)
            k0 = pad + q0 - span * dil
            if dil == 1:
                q_rows = pl.ds(q0, span)
                k_rows = pl.ds(k0, 2 * span)
            else:
                q_rows = pl.ds(q0, span, stride=dil)
                k_rows = pl.ds(k0, 2 * span, stride=dil)
            q2 = _stack_pair_queries(qs[q_rows, :].astype(BF16), lane)
            kb = ks[k_rows, :].astype(BF16)
            vb = vs[k_rows, :].astype(BF16)
            valid = band & (kpos + (n - 1) * span >= 0)
            z = jnp.where(valid, _dot_nt(q2, kb), NEG_INF)
            m = jnp.max(z, axis=-1, keepdims=True)
            p = jnp.exp(z - m)
            den = jnp.sum(p, axis=-1, keepdims=True)
            out = _dot(p.astype(BF16), vb) / den
            lse = m + jnp.log(den)
            og[gi, q_rows, :] = jnp.where(lane < HEAD_DIM, out[:span], out[span:])
            lg[gi, q_rows, :] = jnp.where(lane < HEAD_DIM, lse[:span], lse[span:])
            return 0

        lax.fori_loop(0, dil * nblk, body, 0, unroll=4)

    lse = lg[...]
    top = jnp.max(lse, axis=0)
    wts = jnp.exp(lse - top[None])
    o = jnp.sum(wts * og[...], axis=0) / jnp.sum(wts, axis=0)
    o_ref[0] = _pair_rms_norm(o, g_ref[...], lane).astype(o_ref.dtype)


def _dil_attention(proj, cos_t, sin_t, gain):
    bsz, seq, _ = proj.shape
    pairs = W_DIL // LANES
    pad = DIL_SPAN * max(DILATIONS)
    assert all(seq % (d * DIL_SPAN) == 0 for d in DILATIONS)
    return pl.pallas_call(
        functools.partial(_dil_kernel, seq=seq, pad=pad),
        grid=(bsz, pairs),
        in_specs=[
            pl.BlockSpec((1, seq, LANES), lambda b, p: (b, 0, p)),
            pl.BlockSpec((1, seq, LANES), lambda b, p: (b, 0, pairs + p)),
            pl.BlockSpec((1, seq, LANES), lambda b, p: (b, 0, 2 * pairs + p)),
            pl.BlockSpec((1, seq, LANES), lambda b, p: (b, 0, 0)),
            pl.BlockSpec((1, seq, LANES), lambda b, p: (b, 0, 0)),
            pl.BlockSpec((1, LANES), lambda b, p: (0, p)),
        ],
        out_specs=pl.BlockSpec((1, seq, LANES), lambda b, p: (b, 0, p)),
        out_shape=jax.ShapeDtypeStruct((bsz, seq, W_DIL), BF16),
        scratch_shapes=[
            pltpu.VMEM((seq, LANES), F32),
            pltpu.VMEM((pad + seq, LANES), F32),
            pltpu.VMEM((pad + seq, LANES), F32),
            pltpu.VMEM((len(DILATIONS), seq, LANES), F32),
            pltpu.VMEM((len(DILATIONS), seq, LANES), F32),
        ],
        compiler_params=_params(("arbitrary", "arbitrary")),
        name="dilated_attention",
    )(proj, proj, proj, cos_t, sin_t, gain.reshape(1, W_DIL))


def _oproj_kernel(osb_ref, ofx_ref, odl_ref, w1_ref, w2_ref, w3_ref, x_ref, gate_ref, sc_ref, sh_ref,
                  lng_ref, lnb_ref, wr_ref, rb_ref, x1_ref, h2_ref, h2s_ref, idx_ref, gw_ref, rank_ref,
                  cnt_ref, seen_ref):
    @pl.when((pl.program_id(0) == 0) & (pl.program_id(1) == 0))
    def _():
        seen_ref[...] = jnp.zeros_like(seen_ref)

    mix = (_dot(osb_ref[0], w1_ref[...]) + _dot(ofx_ref[0], w2_ref[...])
           + _dot(odl_ref[0], w3_ref[...]))
    x1 = _layer_norm(DEEPNORM_ALPHA * x_ref[0] + gate_ref[0] * mix, lng_ref[...], lnb_ref[...])
    x1_ref[0] = x1
    h2 = x1 * (1.0 + sc_ref[0]) + sh_ref[0]
    h2_ref[0] = h2
    _store_row_slabs(h2s_ref, _pack_bf16_pairs(h2))

    scores = jax.nn.sigmoid(_dot_f32(h2, wr_ref[...]))
    tm = scores.shape[0]
    lane = lax.broadcasted_iota(I32, (1, LANES), 1)
    lane_f = lane.astype(F32)
    sel = jnp.where(lane < N_EXPERTS, scores + rb_ref[...], NEG_INF)
    idx_acc = jnp.zeros((tm, LANES), I32)
    gate_acc = jnp.zeros((tm, LANES), F32)
    total = jnp.zeros((tm, 1), F32)
    onehot = jnp.zeros((tm, LANES), F32)
    picks = []
    for k in range(TOP_K):
        best = jnp.max(sel, axis=-1, keepdims=True)
        pick = jnp.min(jnp.where(sel == best, lane_f, float(LANES)), axis=-1, keepdims=True)
        chosen = lane_f == pick
        s = jnp.sum(jnp.where(chosen, scores, 0.0), axis=-1, keepdims=True)
        sel = jnp.where(chosen, NEG_INF, sel)
        idx_acc = jnp.where(lane == k, pick.astype(I32), idx_acc)
        gate_acc = jnp.where(lane == k, s, gate_acc)
        onehot = jnp.where(chosen, 1.0, onehot)
        total = total + s
        picks.append(pick)
    idx_ref[0] = idx_acc[:, :8]
    gw_ref[0] = (gate_acc / total * ROUTED_SCALE)[:, :8]

    r = lax.broadcasted_iota(I32, (tm, tm), 0)
    c = lax.broadcasted_iota(I32, (tm, tm), 1)
    earlier = jnp.where(c < r, 1.0, 0.0).astype(BF16)
    rank = _dot(earlier, onehot.astype(BF16)) + seen_ref[...]
    rank_acc = jnp.zeros((tm, LANES), F32)
    for k in range(TOP_K):
        rk = jnp.sum(jnp.where(lane_f == picks[k], rank, 0.0), axis=-1, keepdims=True)
        rank_acc = jnp.where(lane == k, rk, rank_acc)
    rank_ref[0] = rank_acc[:, :8].astype(I32)
    seen_ref[...] = seen_ref[...] + jnp.sum(onehot, axis=0, keepdims=True)
    cnt_ref[...] = seen_ref[...]


def _output_projection(o_sb, o_fx, o_dl, w_out, x, gate1, scale2, shift2, ln_g, ln_b, w_router,
                       router_bias, tm=256):
    bsz, seq, d = x.shape
    pitch = _slab_pitch(d // 2)
    w1 = w_out[:W_SB].astype(BF16)
    w2 = w_out[W_SB:W_SB + W_FOX].astype(BF16)
    w3 = w_out[W_SB + W_FOX:].astype(BF16)
    wr = jnp.zeros((d, LANES), F32).at[:, :N_EXPERTS].set(w_router.astype(F32))
    rb = jnp.zeros((1, LANES), F32).at[0, :N_EXPERTS].set(router_bias.astype(F32))
    tile = lambda w: pl.BlockSpec((1, tm, w), lambda b, i: (b, i, 0))
    whole = lambda a: pl.BlockSpec(a.shape, lambda b, i: (0,) * a.ndim)
    per_batch = pl.BlockSpec((1, 1, d), lambda b, i: (b, 0, 0))
    return pl.pallas_call(
        _oproj_kernel,
        grid=(bsz, seq // tm),
        in_specs=[tile(W_SB), tile(W_FOX), tile(W_DIL), whole(w1), whole(w2), whole(w3), tile(d),
                  per_batch, per_batch, per_batch,
                  pl.BlockSpec((1, d), lambda b, i: (0, 0)), pl.BlockSpec((1, d), lambda b, i: (0, 0)),
                  whole(wr), whole(rb)],
        out_specs=[tile(d), tile(d),
                   pl.BlockSpec((tm * pitch, LANES), lambda b, i: (b * (seq // tm) + i, 0)),
                   tile(8), tile(8), tile(8),
                   pl.BlockSpec((1, LANES), lambda b, i: (0, 0))],
        out_shape=[jax.ShapeDtypeStruct((bsz, seq, d), F32), jax.ShapeDtypeStruct((bsz, seq, d), F32),
                   jax.ShapeDtypeStruct((bsz * seq * pitch, LANES), U32),
                   jax.ShapeDtypeStruct((bsz, seq, 8), I32), jax.ShapeDtypeStruct((bsz, seq, 8), F32),
                   jax.ShapeDtypeStruct((bsz, seq, 8), I32), jax.ShapeDtypeStruct((1, LANES), F32)],
        scratch_shapes=[pltpu.VMEM((1, LANES), F32)],
        compiler_params=_params(("arbitrary", "arbitrary")),
        name="output_projection_router",
    )(o_sb, o_fx, o_dl, w1, w2, w3, x, gate1, scale2, shift2, ln_g.reshape(1, d), ln_b.reshape(1, d),
      wr, rb)


def _dispatch_plan(idx, rank, counts, rows_per_block):
    n_tok = idx.shape[0]
    n_assign = n_tok * TOP_K
    padded = (counts + rows_per_block - 1) // rows_per_block * rows_per_block
    pad_end = jnp.cumsum(padded)
    pad_start = pad_end - padded
    dest_of = (jnp.take(pad_start, idx) + rank).astype(I32)
    n_blocks = -(-(n_assign + N_EXPERTS * (rows_per_block - 1)) // rows_per_block)
    rows = n_blocks * rows_per_block
    tok = jnp.broadcast_to(jnp.arange(n_tok, dtype=I32)[:, None], (n_tok, TOP_K))
    row_tok = jnp.zeros((rows,), I32).at[dest_of.reshape(-1)].set(tok.reshape(-1))
    block_first_row = jnp.arange(n_blocks, dtype=I32) * rows_per_block
    block_e = jnp.sum((pad_end[None, :] <= block_first_row[:, None]).astype(I32), axis=1)
    block_e = jnp.minimum(block_e, N_EXPERTS - 1).astype(I32)
    n_used = (pad_end[-1] // rows_per_block).astype(I32).reshape(1)
    return row_tok, block_e, n_used, dest_of, n_blocks


def _row_copy(src_hbm, first, dst, slot, r, sem, width):
    slabs = width // LANES
    return pltpu.make_async_copy(src_hbm.at[pl.ds(first, slabs), :],
                                 dst.at[slot, pl.ds(r * _slab_pitch(width), slabs), :], sem.at[slot])


def _start_row_gather(src_hbm, first_ref, dst, slot, sem, width, lo, hi):
    def body(p, _):
        r = lo + 2 * p
        _row_copy(src_hbm, first_ref[0, 0, r], dst, slot, r, sem, width).start(priority=0)
        _row_copy(src_hbm, first_ref[0, 0, r + 1], dst, slot, r + 1, sem, width).start(priority=1)
        return 0
    lax.fori_loop(0, (hi - lo) // 2, body, 0, unroll=4)


def _wait_row_gather(src_hbm, dst, slot, sem, width, n):
    def body(r, _):
        _row_copy(src_hbm, 0, dst, slot, r, sem, width).wait()
        return 0
    lax.fori_loop(0, n, body, 0, unroll=8)


def _expert_kernel(be_ref, nu_ref, tok_cur, tok_next, h_hbm, wg_ref, wu_ref, wd_ref, y_ref,
                   xbuf, sem, wg_bf, wu_bf, wd_bf, *, rows):
    width = wg_bf.shape[0] // 2
    i = pl.program_id(0)
    slot = lax.rem(i, 2)
    n_used = nu_ref[0]
    quarter = rows // 4

    def prefetch_next(part):
        @pl.when(i + 1 < n_used)
        def _():
            _start_row_gather(h_hbm, tok_next, xbuf, 1 - slot, sem, width, part * quarter,
                              (part + 1) * quarter)

    @pl.when(i == 0)
    def _():
        _start_row_gather(h_hbm, tok_cur, xbuf, 0, sem, width, 0, rows)

    new_expert = (i == 0) | (be_ref[i] != be_ref[jnp.maximum(i - 1, 0)])

    @pl.when(new_expert & (i < n_used))
    def _():
        wg_bf[...] = wg_ref[0, 0].astype(BF16)
        wu_bf[...] = wu_ref[0, 0].astype(BF16)
        wd_bf[...] = wd_ref[0, 0].astype(BF16)

    @pl.when(i < n_used)
    def _():
        _wait_row_gather(h_hbm, xbuf, slot, sem, width, rows)
        lo, hi = _unpack_bf16_pairs(_load_row_slabs(xbuf.at[slot], rows, width))
        xb = jnp.concatenate([lo.astype(BF16), hi.astype(BF16)], axis=1)
        prefetch_next(0)
        gate = _dot(xb, wg_bf[...])
        prefetch_next(1)
        up = _dot(xb, wu_bf[...])
        prefetch_next(2)
        y = _dot((_silu(gate) * up).astype(BF16), wd_bf[...])
        prefetch_next(3)
        _store_row_slabs(y_ref, _pack_bf16_pairs(y))

    @pl.when(i >= n_used)
    def _():
        y_ref[...] = jnp.zeros_like(y_ref)


def _grouped_experts(h2, row_tok, block_e, n_used, n_blocks, layer, w_gate, w_up, w_down,
                     rows=MOE_ROWS):
    d, hid = w_gate.shape[-2:]
    pitch = _slab_pitch(d // 2)
    tok3 = (row_tok * pitch).reshape(n_blocks, 1, rows)
    smem_rows = lambda shift: pl.BlockSpec(
        (1, 1, rows), lambda i, be, nu: (jnp.minimum(i + shift, n_blocks - 1), 0, 0),
        memory_space=pltpu.SMEM)
    grid_spec = pltpu.PrefetchScalarGridSpec(
        num_scalar_prefetch=2,
        grid=(n_blocks,),
        in_specs=[
            smem_rows(0), smem_rows(1),
            pl.BlockSpec(memory_space=pl.ANY),
            pl.BlockSpec((1, 1, d, hid), lambda i, be, nu: (layer, be[i], 0, 0)),
            pl.BlockSpec((1, 1, d, hid), lambda i, be, nu: (layer, be[i], 0, 0)),
            pl.BlockSpec((1, 1, hid, d), lambda i, be, nu: (layer, be[i], 0, 0)),
        ],
        out_specs=pl.BlockSpec((rows * pitch, LANES), lambda i, be, nu: (i, 0)),
        scratch_shapes=[
            pltpu.VMEM((2, rows * pitch, LANES), U32),
            pltpu.SemaphoreType.DMA((2,)),
            pltpu.VMEM((d, hid), BF16),
            pltpu.VMEM((d, hid), BF16),
            pltpu.VMEM((hid, d), BF16),
        ],
    )
    return pl.pallas_call(
        functools.partial(_expert_kernel, rows=rows),
        grid_spec=grid_spec,
        out_shape=jax.ShapeDtypeStruct((n_blocks * rows * pitch, LANES), U32),
        compiler_params=_params(("arbitrary",)),
        name="grouped_experts",
    )(block_e, n_used, tok3, tok3, h2, w_gate, w_up, w_down)


def _combine_kernel(dest_cur, dest_next, y_hbm, gw_ref, h2_ref, x1_ref, gate_ref, lng_ref, lnb_ref,
                    wg_ref, wu_ref, wd_ref, o_ref, buf, gws, routed_lo, routed_hi, sem, *, tm):
    width = o_ref.shape[1] // 2
    slabs = width // LANES
    pitch = _slab_pitch(width)
    i = pl.program_id(0)
    slot = lax.rem(i, 2)
    n_rows = TOP_K * tm
    third = n_rows // 3

    def prefetch_next(part):
        @pl.when(i + 1 < pl.num_programs(0))
        def _():
            _start_row_gather(y_hbm, dest_next, buf, 1 - slot, sem, width, part * third, (part + 1) * third)

    @pl.when(i == 0)
    def _():
        for s in range(2):
            buf[s, pl.ds(slabs, n_rows, stride=pitch), :] = jnp.zeros((n_rows, LANES), U32)
        _start_row_gather(y_hbm, dest_cur, buf, 0, sem, width, 0, n_rows)

    hb = h2_ref[...].astype(BF16)
    prefetch_next(0)
    gate = _dot(hb, wg_ref[...])
    prefetch_next(1)
    up = _dot(hb, wu_ref[...])
    prefetch_next(2)
    ffn = _dot((_silu(gate) * up).astype(BF16), wd_ref[...])

    _wait_row_gather(y_hbm, buf, slot, sem, width, n_rows)
    gw = gw_ref[...]
    for c in range(pitch):
        gws[pl.ds(c, tm, stride=pitch), :] = gw
    gw = gws[...]
    span = tm * pitch
    acc_lo = acc_hi = None
    for k in range(TOP_K):
        lo, hi = _unpack_bf16_pairs(buf[slot, k * span:(k + 1) * span, :])
        g = gw[:, k:k + 1]
        acc_lo = g * lo if acc_lo is None else acc_lo + g * lo
        acc_hi = g * hi if acc_hi is None else acc_hi + g * hi
    routed_lo[...] = acc_lo
    routed_hi[...] = acc_hi
    ffn = ffn + jnp.concatenate([_load_row_slabs(routed_lo, tm, width), _load_row_slabs(routed_hi, tm, width)],
                                axis=1)
    o_ref[...] = _layer_norm(DEEPNORM_ALPHA * x1_ref[...] + gate_ref[0] * ffn, lng_ref[...], lnb_ref[...])


def _combine(y, dest_of, gates, h2, x1, gate2, ln_g, ln_b, ws_gate, ws_up, ws_down, seq,
             tm=COMBINE_TOKENS):
    n_tok, d = h2.shape
    n_tiles = n_tok // tm
    hid = ws_gate.shape[-1]
    pitch = _slab_pitch(d // 2)
    dest3 = (dest_of * pitch).reshape(n_tiles, tm, TOP_K).transpose(0, 2, 1).reshape(n_tiles, 1, TOP_K * tm)
    smem_rows = lambda shift: pl.BlockSpec(
        (1, 1, TOP_K * tm), lambda i: (jnp.minimum(i + shift, n_tiles - 1), 0, 0),
        memory_space=pltpu.SMEM)
    tile = pl.BlockSpec((tm, d), lambda i: (i, 0))
    row = pl.BlockSpec((1, d), lambda i: (0, 0))
    return pl.pallas_call(
        functools.partial(_combine_kernel, tm=tm),
        grid=(n_tiles,),
        in_specs=[
            smem_rows(0), smem_rows(1),
            pl.BlockSpec(memory_space=pl.ANY),
            pl.BlockSpec((tm, 8), lambda i: (i, 0)),
            tile, tile,
            pl.BlockSpec((1, 1, d), lambda i: (i * tm // seq, 0, 0)),
            row, row,
            pl.BlockSpec((d, hid), lambda i: (0, 0)),
            pl.BlockSpec((d, hid), lambda i: (0, 0)),
            pl.BlockSpec((hid, d), lambda i: (0, 0)),
        ],
        out_specs=tile,
        out_shape=jax.ShapeDtypeStruct((n_tok, d), F32),
        scratch_shapes=[pltpu.VMEM((2, TOP_K * tm * pitch, LANES), U32),
                        pltpu.VMEM((tm * pitch, 8), F32),
                        pltpu.VMEM((tm * pitch, LANES), F32), pltpu.VMEM((tm * pitch, LANES), F32),
                        pltpu.SemaphoreType.DMA((2,))],
        compiler_params=_params(("arbitrary",)),
        name="combine_shared_layernorm",
    )(dest3, dest3, y, gates, h2, x1, gate2, ln_g.reshape(1, d), ln_b.reshape(1, d),
      ws_gate.astype(BF16), ws_up.astype(BF16), ws_down.astype(BF16))


def kernel(x, c, positions, w_ada, b_ada, w_in, b_forget, head_norm_g, w_out, ln1_g, ln1_b, w_router,
           router_bias, w_exp_gate, w_exp_up, w_exp_down, w_sh_gate, w_sh_up, w_sh_down, ln2_g, ln2_b):
    bsz, seq, d = x.shape
    n_tok = bsz * seq
    mod = _ada_modulation(c, w_ada, b_ada)
    cos_t, sin_t = _rope_tables(positions)
    n_a = 3 * (W_SB + W_FOX)
    n_qkv = n_a + 3 * W_DIL
    for l in range(DEPTH):
        shift1, scale1, gate1, shift2, scale2, gate2 = (
            mod[l, :, k * d:(k + 1) * d].reshape(bsz, 1, d) for k in range(6))
        w_a = w_in[l, :, :n_a].astype(BF16)
        w_f = jnp.zeros((d, 2 * LANES), F32).at[:, :N_HEADS_FOX].set(w_in[l, :, n_qkv:])
        w_b = jnp.concatenate([w_in[l, :, n_a:n_qkv], w_f], axis=1).astype(BF16)
        proj_a = _modulated_matmul(x, scale1, shift1, w_a, BF16, tm=1024, tn=n_a // 3)
        proj_b = _modulated_matmul(x, scale1, shift1, w_b, F32, tm=1024, tn=w_b.shape[1] // 2)
        g = head_norm_g[l]
        o_sb = _sb_attention(proj_a, g[:W_SB])
        cum_col, cum_row = _fox_prefix(proj_b, 3 * W_DIL // LANES, b_forget[l])
        cum_row = cum_row[:, :N_HEADS_FOX].reshape(bsz, N_HEADS_FOX // 2, 2, seq)
        o_fx = _fox_attention(proj_a, 3 * W_SB // LANES, cum_col, cum_row, g[W_SB:W_SB + W_FOX])
        o_dl = _dil_attention(proj_b, cos_t, sin_t, g[W_SB + W_FOX:])
        x1, h2, h2_slabs, idx, gates, rank, counts = _output_projection(
            o_sb, o_fx, o_dl, w_out[l], x, gate1, scale2, shift2, ln1_g[l], ln1_b[l], w_router[l],
            router_bias[l])
        h2 = h2.reshape(n_tok, d)
        row_tok, block_e, n_used, dest_of, n_blocks = _dispatch_plan(
            idx.reshape(n_tok, 8)[:, :TOP_K], rank.reshape(n_tok, 8)[:, :TOP_K],
            counts[0, :N_EXPERTS].astype(I32), MOE_ROWS)
        y = _grouped_experts(h2_slabs, row_tok, block_e, n_used, n_blocks, l,
                             w_exp_gate, w_exp_up, w_exp_down)
        x = _combine(y, dest_of, gates.reshape(n_tok, 8), h2, x1.reshape(n_tok, d), gate2, ln2_g[l],
                     ln2_b[l], w_sh_gate[l], w_sh_up[l], w_sh_down[l], seq).reshape(bsz, seq, d)
    return x
```

```python
import functools

import jax
import jax.numpy as jnp
from jax import lax
from jax.experimental import pallas as pl
from jax.experimental.pallas import tpu as pltpu

F32 = jnp.float32
BF16 = jnp.bfloat16
I32 = jnp.int32

D_MODEL = 2048
DEPTH = 2
HEAD_DIM = 64
N_HEADS_SB = 8
N_HEADS_FOX = 12
N_HEADS_DIL = 12
W_SB = N_HEADS_SB * HEAD_DIM
W_FOX = N_HEADS_FOX * HEAD_DIM
W_DIL = N_HEADS_DIL * HEAD_DIM
DILATIONS = (1, 4, 16)
DIL_SPAN = 128
ROPE_THETA = 10000.0
N_EXPERTS = 64
TOP_K = 6
ROUTED_SCALE = 2.5
DEEPNORM_ALPHA = (2 * DEPTH) ** 0.25
LN_EPS = 1e-5
RMS_EPS = 1e-6
QK_SCALE = HEAD_DIM ** -0.5
NEG_INF = float("-inf")

LANES = 128
ATTN_QUERY_BLOCK = 512
ATTN_KEY_BLOCK = 256
MOE_ROWS = 256
COMBINE_TOKENS = 128
VMEM_LIMIT = 56 * 1024 * 1024


def _params(semantics, vmem=VMEM_LIMIT):
    return pltpu.CompilerParams(dimension_semantics=semantics, vmem_limit_bytes=vmem)


def _dot(a, b):
    return jnp.dot(a, b, preferred_element_type=F32)


def _dot_nt(a, b):
    return lax.dot_general(a, b, (((1,), (1,)), ((), ())), preferred_element_type=F32)


def _split2(a):
    hi = a.astype(BF16)
    lo = (a - hi.astype(F32)).astype(BF16)
    return hi, lo


def _split3(a):
    hi = a.astype(BF16)
    r1 = a - hi.astype(F32)
    mid = r1.astype(BF16)
    lo = (r1 - mid.astype(F32)).astype(BF16)
    return hi, mid, lo


def _dot_f32(a, b):
    ah, al = _split2(a)
    bh, bl = _split2(b)
    return _dot(ah, bh) + _dot(ah, bl) + _dot(al, bh)


def _softplus(z):
    return jnp.maximum(z, 0.0) + jnp.log1p(jnp.exp(-jnp.abs(z)))


def _silu(z):
    return z * jax.nn.sigmoid(z)


def _layer_norm(r, g, b):
    mu = jnp.mean(r, axis=-1, keepdims=True)
    d = r - mu
    var = jnp.mean(d * d, axis=-1, keepdims=True)
    return d * lax.rsqrt(var + LN_EPS) * g + b


def _slab_pitch(width):
    return width // LANES + 1


def _store_row_slabs(dst, value):
    n, width = value.shape
    slabs = width // LANES
    pitch = _slab_pitch(width)
    for c in range(slabs):
        dst[pl.ds(c, n, stride=pitch), :] = value[:, c * LANES:(c + 1) * LANES]
    dst[pl.ds(slabs, n, stride=pitch), :] = jnp.zeros((n, LANES), dst.dtype)


def _load_row_slabs(src, n, width):
    pitch = _slab_pitch(width)
    return jnp.concatenate([src[pl.ds(c, n, stride=pitch), :] for c in range(width // LANES)], axis=1)


def _pair_rms_norm(o, g, lane):
    left = lane < HEAD_DIM
    sq = o * o
    s0 = jnp.sum(jnp.where(left, sq, 0.0), axis=-1, keepdims=True)
    s1 = jnp.sum(jnp.where(left, 0.0, sq), axis=-1, keepdims=True)
    ms = jnp.where(left, s0, s1) * (1.0 / HEAD_DIM)
    return o * lax.rsqrt(ms + RMS_EPS) * g


def _ada_kernel(c_ref, w_ref, b_ref, o_ref):
    o_ref[0] = _dot_f32(_silu(c_ref[...]), w_ref[0]) + b_ref[0]


def _ada_modulation(c, w_ada, b_ada, tn=1024):
    depth, d, n = w_ada.shape
    bsz = c.shape[0]
    return pl.pallas_call(
        _ada_kernel,
        grid=(depth, n // tn),
        in_specs=[
            pl.BlockSpec((bsz, d), lambda l, j: (0, 0)),
            pl.BlockSpec((1, d, tn), lambda l, j: (l, 0, j)),
            pl.BlockSpec((1, 1, tn), lambda l, j: (l, 0, j)),
        ],
        out_specs=pl.BlockSpec((1, bsz, tn), lambda l, j: (l, 0, j)),
        out_shape=jax.ShapeDtypeStruct((depth, bsz, n), F32),
        compiler_params=_params(("arbitrary", "arbitrary")),
        name="ada_modulation",
    )(c, w_ada, b_ada.reshape(depth, 1, n))


def _rope_kernel(pos_ref, invf_ref, sign_ref, cos_ref, sin_ref):
    ang = pos_ref[0].astype(F32) * invf_ref[...]
    cos_ref[0] = jnp.cos(ang)
    sin_ref[0] = jnp.sin(ang) * sign_ref[...]


def _rope_tables(positions, ts=512):
    bsz, seq = positions.shape
    half = HEAD_DIM // 2
    inv_freq = ROPE_THETA ** (-jnp.arange(half, dtype=F32) / half)
    inv_lane = jnp.tile(inv_freq, LANES // half).reshape(1, LANES)
    sign = jnp.tile(jnp.concatenate([-jnp.ones((half,), F32), jnp.ones((half,), F32)]),
                    LANES // HEAD_DIM).reshape(1, LANES)
    tab = jax.ShapeDtypeStruct((bsz, seq, LANES), F32)
    return pl.pallas_call(
        _rope_kernel,
        grid=(bsz, seq // ts),
        in_specs=[
            pl.BlockSpec((1, ts, 1), lambda b, i: (b, i, 0)),
            pl.BlockSpec((1, LANES), lambda b, i: (0, 0)),
            pl.BlockSpec((1, LANES), lambda b, i: (0, 0)),
        ],
        out_specs=[pl.BlockSpec((1, ts, LANES), lambda b, i: (b, i, 0))] * 2,
        out_shape=[tab, tab],
        compiler_params=_params(("arbitrary", "arbitrary")),
        name="rope_tables",
    )(positions.reshape(bsz, seq, 1), inv_lane, sign)


def _modmm_kernel(x_ref, sc_ref, sh_ref, w_ref, o_ref):
    h = x_ref[0] * (1.0 + sc_ref[0]) + sh_ref[0]
    o_ref[0] = _dot(h.astype(BF16), w_ref[...]).astype(o_ref.dtype)


def _modulated_matmul(x, scale, shift, w, out_dtype, tm, tn):
    bsz, seq, d = x.shape
    n = w.shape[1]
    return pl.pallas_call(
        _modmm_kernel,
        grid=(n // tn, bsz, seq // tm),
        in_specs=[
            pl.BlockSpec((1, tm, d), lambda j, b, i: (b, i, 0)),
            pl.BlockSpec((1, 1, d), lambda j, b, i: (b, 0, 0)),
            pl.BlockSpec((1, 1, d), lambda j, b, i: (b, 0, 0)),
            pl.BlockSpec((d, tn), lambda j, b, i: (0, j)),
        ],
        out_specs=pl.BlockSpec((1, tm, tn), lambda j, b, i: (b, i, j)),
        out_shape=jax.ShapeDtypeStruct((bsz, seq, n), out_dtype),
        compiler_params=_params(("arbitrary", "arbitrary", "arbitrary")),
        name="modulated_projection",
    )(x, scale, shift, w)


def _stack_pair_queries(q, lane):
    zero = jnp.zeros_like(q)
    return jnp.concatenate([jnp.where(lane < HEAD_DIM, q, zero), jnp.where(lane < HEAD_DIM, zero, q)], axis=0)


def _sb_kernel(q_ref, k_ref, v_ref, g_ref, o_ref, *, tq, tk):
    qi = pl.program_id(2)
    sub = tq // tk
    lane = lax.broadcasted_iota(I32, (1, LANES), 1)
    row = lax.broadcasted_iota(I32, (2 * tq, tk), 0) & (tq - 1)
    col = lax.broadcasted_iota(I32, (2 * tq, tk), 1)
    r2 = lax.broadcasted_iota(I32, (tk, tk), 0)
    c2 = lax.broadcasted_iota(I32, (tk, tk), 1)
    later_keys = jnp.where(r2 > c2, 1.0, 0.0).astype(BF16)
    later_keys = jnp.concatenate([later_keys, later_keys], axis=0)
    q2 = _stack_pair_queries(q_ref[0] * QK_SCALE, lane)

    def block(kj, carry, diag):
        o, tail = carry
        start = pl.multiple_of(kj * tk, tk)
        kb = k_ref[0, pl.ds(start, tk), :]
        vb = v_ref[0, pl.ds(start, tk), :]
        z = _dot_nt(q2, kb)
        drop = jnp.log(1.0 + jnp.exp(-jnp.abs(z))) + jnp.maximum(z, 0.0)
        if diag is not None:
            past = col + diag * tk < row
            drop = jnp.where(past, drop, 0.0)
        hi, lo = _split2(drop)
        later = _dot(jnp.concatenate([hi, lo], axis=1), later_keys) + tail
        w = jnp.exp(z - drop - later)
        if diag is not None:
            w = jnp.where(past, w, 0.0)
        o = o + _dot(w.astype(BF16), vb)
        tail = tail + jnp.sum(drop, axis=-1, keepdims=True)
        return o, tail

    def earlier_blocks(it, carry):
        base = (qi - 1 - it) * sub
        for j in reversed(range(sub)):
            carry = block(base + j, carry, None)
        return carry

    carry = (jnp.zeros((2 * tq, LANES), F32), jnp.zeros((2 * tq, 1), F32))
    for diag in reversed(range(sub)):
        carry = block(qi * sub + diag, carry, diag)
    carry = lax.fori_loop(0, qi, earlier_blocks, carry)
    o = jnp.where(lane < HEAD_DIM, carry[0][:tq], carry[0][tq:])
    o_ref[0] = _pair_rms_norm(o, g_ref[...], lane).astype(o_ref.dtype)


def _sb_attention(proj, gain, blk=ATTN_QUERY_BLOCK):
    bsz, seq, _ = proj.shape
    pairs = W_SB // LANES
    return pl.pallas_call(
        functools.partial(_sb_kernel, tq=blk, tk=ATTN_KEY_BLOCK),
        grid=(bsz, pairs, seq // blk),
        in_specs=[
            pl.BlockSpec((1, blk, LANES), lambda b, p, i: (b, i, p)),
            pl.BlockSpec((1, seq, LANES), lambda b, p, i: (b, 0, pairs + p)),
            pl.BlockSpec((1, seq, LANES), lambda b, p, i: (b, 0, 2 * pairs + p)),
            pl.BlockSpec((1, LANES), lambda b, p, i: (0, p)),
        ],
        out_specs=pl.BlockSpec((1, blk, LANES), lambda b, p, i: (b, i, p)),
        out_shape=jax.ShapeDtypeStruct((bsz, seq, W_SB), BF16),
        compiler_params=_params(("arbitrary", "arbitrary", "arbitrary")),
        name="stick_breaking_attention",
    )(proj, proj, proj, gain.reshape(1, W_SB))


def _fox_prep_kernel(f_ref, b_ref, col_ref, row_ref, *, blk):
    seq = f_ref.shape[1]
    x = f_ref[0] + b_ref[...]
    log_f = -_softplus(-x)
    r = lax.broadcasted_iota(I32, (blk, blk), 0)
    c = lax.broadcasted_iota(I32, (blk, blk), 1)
    upto = jnp.where(c <= r, 1.0, 0.0).astype(BF16)
    carry = jnp.zeros((1, LANES), F32)
    parts = []
    for i in range(seq // blk):
        hi, mid, lo = _split3(log_f[i * blk:(i + 1) * blk])
        cs = _dot(upto, hi) + _dot(upto, mid) + _dot(upto, lo) + carry
        parts.append(cs)
        carry = cs[blk - 1:blk, :]
    cum = jnp.concatenate(parts, axis=0)
    col_ref[0] = cum
    row_ref[0] = cum.T


def _fox_prefix(f_logits, f_block, b_forget):
    bsz, seq, _ = f_logits.shape
    bias = jnp.zeros((1, LANES), F32).at[0, :N_HEADS_FOX].set(b_forget)
    return pl.pallas_call(
        functools.partial(_fox_prep_kernel, blk=256),
        grid=(bsz,),
        in_specs=[
            pl.BlockSpec((1, seq, LANES), lambda b: (b, 0, f_block)),
            pl.BlockSpec((1, LANES), lambda b: (0, 0)),
        ],
        out_specs=[
            pl.BlockSpec((1, seq, LANES), lambda b: (b, 0, 0)),
            pl.BlockSpec((1, LANES, seq), lambda b: (b, 0, 0)),
        ],
        out_shape=[jax.ShapeDtypeStruct((bsz, seq, LANES), F32),
                   jax.ShapeDtypeStruct((bsz, LANES, seq), F32)],
        compiler_params=_params(("arbitrary",)),
        name="forget_prefix",
    )(f_logits, bias)


def _fox_kernel(q_ref, k_ref, v_ref, cq_ref, ck_ref, g_ref, o_ref, *, tq, tk):
    pair = pl.program_id(1)
    qi = pl.program_id(2)
    sub = tq // tk
    lane = lax.broadcasted_iota(I32, (1, LANES), 1)
    row = lax.broadcasted_iota(I32, (2 * tq, tk), 0) & (tq - 1)
    col = lax.broadcasted_iota(I32, (2 * tq, tk), 1)
    q2 = _stack_pair_queries(q_ref[0] * QK_SCALE, lane)
    cum_q_all = cq_ref[0]
    cum_q = [jnp.sum(jnp.where(lane == 2 * pair + h, cum_q_all, 0.0), axis=-1, keepdims=True)
             for h in range(2)]

    def block(kj, size, carry, diag):
        m, l, o = carry
        start = pl.multiple_of(kj * size, size)
        kb = k_ref[0, pl.ds(start, size), :]
        vb = v_ref[0, pl.ds(start, size), :]
        qk = _dot_nt(q2, kb)
        z = jnp.concatenate(
            [qk[h * tq:(h + 1) * tq] + (cum_q[h] - ck_ref[0, 0, h:h + 1, pl.ds(start, size)])
             for h in range(2)], axis=0)
        if diag is not None:
            z = jnp.where(col + diag * tk <= row, z, NEG_INF)
        m_new = jnp.maximum(m, jnp.max(z, axis=-1, keepdims=True))
        p = jnp.exp(z - m_new)
        a = jnp.exp(m - m_new)
        l = a * l + jnp.sum(p, axis=-1, keepdims=True)
        o = a * o + _dot(p.astype(BF16), vb)
        return m_new, l, o

    carry = (jnp.full((2 * tq, 1), NEG_INF, F32), jnp.zeros((2 * tq, 1), F32),
             jnp.zeros((2 * tq, LANES), F32))
    for diag in range(sub):
        carry = block(qi * sub + diag, tk, carry, diag)
    carry = lax.fori_loop(0, qi, lambda kj, c: block(kj, tq, c, None), carry)
    o = carry[2] / carry[1]
    o = jnp.where(lane < HEAD_DIM, o[:tq], o[tq:])
    o_ref[0] = _pair_rms_norm(o, g_ref[...], lane).astype(o_ref.dtype)


def _fox_attention(proj, col_base, cum_col, cum_row, gain, blk=ATTN_QUERY_BLOCK):
    bsz, seq, _ = proj.shape
    pairs = W_FOX // LANES
    return pl.pallas_call(
        functools.partial(_fox_kernel, tq=blk, tk=ATTN_KEY_BLOCK),
        grid=(bsz, pairs, seq // blk),
        in_specs=[
            pl.BlockSpec((1, blk, LANES), lambda b, p, i: (b, i, col_base + p)),
            pl.BlockSpec((1, seq, LANES), lambda b, p, i: (b, 0, col_base + pairs + p)),
            pl.BlockSpec((1, seq, LANES), lambda b, p, i: (b, 0, col_base + 2 * pairs + p)),
            pl.BlockSpec((1, blk, LANES), lambda b, p, i: (b, i, 0)),
            pl.BlockSpec((1, 1, 2, seq), lambda b, p, i: (b, p, 0, 0)),
            pl.BlockSpec((1, LANES), lambda b, p, i: (0, p)),
        ],
        out_specs=pl.BlockSpec((1, blk, LANES), lambda b, p, i: (b, i, p)),
        out_shape=jax.ShapeDtypeStruct((bsz, seq, W_FOX), BF16),
        compiler_params=_params(("arbitrary", "arbitrary", "arbitrary")),
        name="forgetting_attention",
    )(proj, proj, proj, cum_col, cum_row, gain.reshape(1, W_FOX))


def _dil_kernel(q_ref, k_ref, v_ref, cos_ref, sin_ref, g_ref, o_ref,
                qs, ks, vs, og, lg, *, seq, pad):
    lane = lax.broadcasted_iota(I32, (1, LANES), 1)
    first_half = (lane & (HEAD_DIM - 1)) < HEAD_DIM // 2
    cos = cos_ref[0]
    sin = sin_ref[0]

    def rope(t):
        partner = jnp.where(first_half, pltpu.roll(t, LANES - HEAD_DIM // 2, 1),
                            pltpu.roll(t, HEAD_DIM // 2, 1))
        return t * cos + partner * sin

    qs[...] = rope(q_ref[0]) * QK_SCALE
    ks[0:pad, :] = jnp.zeros((pad, LANES), F32)
    vs[0:pad, :] = jnp.zeros((pad, LANES), F32)
    ks[pad:pad + seq, :] = rope(k_ref[0])
    vs[pad:pad + seq, :] = v_ref[0]

    span = DIL_SPAN
    qpos = lax.broadcasted_iota(I32, (2 * span, 2 * span), 0) & (span - 1)
    kpos = lax.broadcasted_iota(I32, (2 * span, 2 * span), 1)
    band = (kpos >= qpos) & (kpos <= qpos + span)

    for gi, dil in enumerate(DILATIONS):
        nblk = seq // dil // span

        def body(idx, _, gi=gi, dil=dil):
            r = lax.rem(idx, dil)
            n = lax.div(idx, dil)
            q0 = r + n * (span * dil)
            k0 = pad + q0 - span * dil
            if dil == 1:
                q_rows = pl.ds(q0, span)
                k_rows = pl.ds(k0, 2 * span)
            else:
                q_rows = pl.ds(q0, span, stride=dil)
                k_rows = pl.ds(k0, 2 * span, stride=dil)
            q2 = _stack_pair_queries(qs[q_rows, :].astype(BF16), lane)
            kb = ks[k_rows, :].astype(BF16)
            vb = vs[k_rows, :].astype(BF16)
            valid = band & (kpos + (n - 1) * span >= 0)
            z = jnp.where(valid, _dot_nt(q2, kb), NEG_INF)
            m = jnp.max(z, axis=-1, keepdims=True)
            p = jnp.exp(z - m)
            den = jnp.sum(p, axis=-1, keepdims=True)
            out = _dot(p.astype(BF16), vb) / den
            lse = m + jnp.log(den)
            og[gi, q_rows, :] = jnp.where(lane < HEAD_DIM, out[:span], out[span:])
            lg[gi, q_rows, :] = jnp.where(lane < HEAD_DIM, lse[:span], lse[span:])
            return 0

        lax.fori_loop(0, dil * nblk, body, 0, unroll=4)

    lse = lg[...]
    top = jnp.max(lse, axis=0)
    wts = jnp.exp(lse - top[None])
    o = jnp.sum(wts * og[...], axis=0) / jnp.sum(wts, axis=0)
    o_ref[0] = _pair_rms_norm(o, g_ref[...], lane).astype(o_ref.dtype)


def _dil_attention(proj, cos_t, sin_t, gain):
    bsz, seq, _ = proj.shape
    pairs = W_DIL // LANES
    pad = DIL_SPAN * max(DILATIONS)
    assert all(seq % (d * DIL_SPAN) == 0 for d in DILATIONS)
    return pl.pallas_call(
        functools.partial(_dil_kernel, seq=seq, pad=pad),
        grid=(bsz, pairs),
        in_specs=[
            pl.BlockSpec((1, seq, LANES), lambda b, p: (b, 0, p)),
            pl.BlockSpec((1, seq, LANES), lambda b, p: (b, 0, pairs + p)),
            pl.BlockSpec((1, seq, LANES), lambda b, p: (b, 0, 2 * pairs + p)),
            pl.BlockSpec((1, seq, LANES), lambda b, p: (b, 0, 0)),
            pl.BlockSpec((1, seq, LANES), lambda b, p: (b, 0, 0)),
            pl.BlockSpec((1, LANES), lambda b, p: (0, p)),
        ],
        out_specs=pl.BlockSpec((1, seq, LANES), lambda b, p: (b, 0, p)),
        out_shape=jax.ShapeDtypeStruct((bsz, seq, W_DIL), BF16),
        scratch_shapes=[
            pltpu.VMEM((seq, LANES), F32),
            pltpu.VMEM((pad + seq, LANES), F32),
            pltpu.VMEM((pad + seq, LANES), F32),
            pltpu.VMEM((len(DILATIONS), seq, LANES), F32),
            pltpu.VMEM((len(DILATIONS), seq, LANES), F32),
        ],
        compiler_params=_params(("arbitrary", "arbitrary")),
        name="dilated_attention",
    )(proj, proj, proj, cos_t, sin_t, gain.reshape(1, W_DIL))


def _oproj_kernel(osb_ref, ofx_ref, odl_ref, w1_ref, w2_ref, w3_ref, x_ref, gate_ref, sc_ref, sh_ref,
                  lng_ref, lnb_ref, wr_ref, rb_ref, x1_ref, h2_ref, h2s_ref, idx_ref, gw_ref, rank_ref,
                  cnt_ref, seen_ref):
    @pl.when((pl.program_id(0) == 0) & (pl.program_id(1) == 0))
    def _():
        seen_ref[...] = jnp.zeros_like(seen_ref)

    mix = (_dot(osb_ref[0], w1_ref[...]) + _dot(ofx_ref[0], w2_ref[...])
           + _dot(odl_ref[0], w3_ref[...]))
    x1 = _layer_norm(DEEPNORM_ALPHA * x_ref[0] + gate_ref[0] * mix, lng_ref[...], lnb_ref[...])
    x1_ref[0] = x1
    h2 = x1 * (1.0 + sc_ref[0]) + sh_ref[0]
    h2_ref[0] = h2
    _store_row_slabs(h2s_ref, h2)

    scores = jax.nn.sigmoid(_dot_f32(h2, wr_ref[...]))
    tm = scores.shape[0]
    lane = lax.broadcasted_iota(I32, (1, LANES), 1)
    lane_f = lane.astype(F32)
    sel = jnp.where(lane < N_EXPERTS, scores + rb_ref[...], NEG_INF)
    idx_acc = jnp.zeros((tm, LANES), I32)
    gate_acc = jnp.zeros((tm, LANES), F32)
    total = jnp.zeros((tm, 1), F32)
    onehot = jnp.zeros((tm, LANES), F32)
    picks = []
    for k in range(TOP_K):
        best = jnp.max(sel, axis=-1, keepdims=True)
        pick = jnp.min(jnp.where(sel == best, lane_f, float(LANES)), axis=-1, keepdims=True)
        chosen = lane_f == pick
        s = jnp.sum(jnp.where(chosen, scores, 0.0), axis=-1, keepdims=True)
        sel = jnp.where(chosen, NEG_INF, sel)
        idx_acc = jnp.where(lane == k, pick.astype(I32), idx_acc)
        gate_acc = jnp.where(lane == k, s, gate_acc)
        onehot = jnp.where(chosen, 1.0, onehot)
        total = total + s
        picks.append(pick)
    idx_ref[0] = idx_acc[:, :8]
    gw_ref[0] = (gate_acc / total * ROUTED_SCALE)[:, :8]

    r = lax.broadcasted_iota(I32, (tm, tm), 0)
    c = lax.broadcasted_iota(I32, (tm, tm), 1)
    earlier = jnp.where(c < r, 1.0, 0.0).astype(BF16)
    rank = _dot(earlier, onehot.astype(BF16)) + seen_ref[...]
    rank_acc = jnp.zeros((tm, LANES), F32)
    for k in range(TOP_K):
        rk = jnp.sum(jnp.where(lane_f == picks[k], rank, 0.0), axis=-1, keepdims=True)
        rank_acc = jnp.where(lane == k, rk, rank_acc)
    rank_ref[0] = rank_acc[:, :8].astype(I32)
    seen_ref[...] = seen_ref[...] + jnp.sum(onehot, axis=0, keepdims=True)
    cnt_ref[...] = seen_ref[...]


def _output_projection(o_sb, o_fx, o_dl, w_out, x, gate1, scale2, shift2, ln_g, ln_b, w_router,
                       router_bias, tm=256):
    bsz, seq, d = x.shape
    pitch = _slab_pitch(d)
    w1 = w_out[:W_SB].astype(BF16)
    w2 = w_out[W_SB:W_SB + W_FOX].astype(BF16)
    w3 = w_out[W_SB + W_FOX:].astype(BF16)
    wr = jnp.zeros((d, LANES), F32).at[:, :N_EXPERTS].set(w_router.astype(F32))
    rb = jnp.zeros((1, LANES), F32).at[0, :N_EXPERTS].set(router_bias.astype(F32))
    tile = lambda w: pl.BlockSpec((1, tm, w), lambda b, i: (b, i, 0))
    whole = lambda a: pl.BlockSpec(a.shape, lambda b, i: (0,) * a.ndim)
    per_batch = pl.BlockSpec((1, 1, d), lambda b, i: (b, 0, 0))
    return pl.pallas_call(
        _oproj_kernel,
        grid=(bsz, seq // tm),
        in_specs=[tile(W_SB), tile(W_FOX), tile(W_DIL), whole(w1), whole(w2), whole(w3), tile(d),
                  per_batch, per_batch, per_batch,
                  pl.BlockSpec((1, d), lambda b, i: (0, 0)), pl.BlockSpec((1, d), lambda b, i: (0, 0)),
                  whole(wr), whole(rb)],
        out_specs=[tile(d), tile(d),
                   pl.BlockSpec((tm * pitch, LANES), lambda b, i: (b * (seq // tm) + i, 0)),
                   tile(8), tile(8), tile(8),
                   pl.BlockSpec((1, LANES), lambda b, i: (0, 0))],
        out_shape=[jax.ShapeDtypeStruct((bsz, seq, d), F32), jax.ShapeDtypeStruct((bsz, seq, d), F32),
                   jax.ShapeDtypeStruct((bsz * seq * pitch, LANES), F32),
                   jax.ShapeDtypeStruct((bsz, seq, 8), I32), jax.ShapeDtypeStruct((bsz, seq, 8), F32),
                   jax.ShapeDtypeStruct((bsz, seq, 8), I32), jax.ShapeDtypeStruct((1, LANES), F32)],
        scratch_shapes=[pltpu.VMEM((1, LANES), F32)],
        compiler_params=_params(("arbitrary", "arbitrary")),
        name="output_projection_router",
    )(o_sb, o_fx, o_dl, w1, w2, w3, x, gate1, scale2, shift2, ln_g.reshape(1, d), ln_b.reshape(1, d),
      wr, rb)


def _dispatch_plan(idx, rank, counts, rows_per_block):
    n_tok = idx.shape[0]
    n_assign = n_tok * TOP_K
    padded = (counts + rows_per_block - 1) // rows_per_block * rows_per_block
    pad_end = jnp.cumsum(padded)
    pad_start = pad_end - padded
    dest_of = (jnp.take(pad_start, idx) + rank).astype(I32)
    n_blocks = -(-(n_assign + N_EXPERTS * (rows_per_block - 1)) // rows_per_block)
    rows = n_blocks * rows_per_block
    tok = jnp.broadcast_to(jnp.arange(n_tok, dtype=I32)[:, None], (n_tok, TOP_K))
    row_tok = jnp.zeros((rows,), I32).at[dest_of.reshape(-1)].set(tok.reshape(-1))
    block_first_row = jnp.arange(n_blocks, dtype=I32) * rows_per_block
    block_e = jnp.sum((pad_end[None, :] <= block_first_row[:, None]).astype(I32), axis=1)
    block_e = jnp.minimum(block_e, N_EXPERTS - 1).astype(I32)
    n_used = (pad_end[-1] // rows_per_block).astype(I32).reshape(1)
    return row_tok, block_e, n_used, dest_of, n_blocks


def _row_copy(src_hbm, first, dst, slot, r, sem, width):
    slabs = width // LANES
    return pltpu.make_async_copy(src_hbm.at[pl.ds(first, slabs), :],
                                 dst.at[slot, pl.ds(r * _slab_pitch(width), slabs), :], sem.at[slot])


def _start_row_gather(src_hbm, first_ref, dst, slot, sem, width, n):
    def body(r, _):
        _row_copy(src_hbm, first_ref[0, 0, r], dst, slot, r, sem, width).start()
        return 0
    lax.fori_loop(0, n, body, 0, unroll=8)


def _wait_row_gather(src_hbm, dst, slot, sem, width, n):
    def body(r, _):
        _row_copy(src_hbm, 0, dst, slot, r, sem, width).wait()
        return 0
    lax.fori_loop(0, n, body, 0, unroll=8)


def _expert_kernel(be_ref, nu_ref, tok_cur, tok_next, h_hbm, wg_ref, wu_ref, wd_ref, y_ref,
                   xbuf, sem, wg_bf, wu_bf, wd_bf, *, rows):
    width = wg_bf.shape[0]
    i = pl.program_id(0)
    slot = lax.rem(i, 2)
    n_used = nu_ref[0]

    @pl.when(i == 0)
    def _():
        _start_row_gather(h_hbm, tok_cur, xbuf, 0, sem, width, rows)

    @pl.when(i + 1 < n_used)
    def _():
        _start_row_gather(h_hbm, tok_next, xbuf, 1 - slot, sem, width, rows)

    new_expert = (i == 0) | (be_ref[i] != be_ref[jnp.maximum(i - 1, 0)])

    @pl.when(new_expert & (i < n_used))
    def _():
        wg_bf[...] = wg_ref[0, 0].astype(BF16)
        wu_bf[...] = wu_ref[0, 0].astype(BF16)
        wd_bf[...] = wd_ref[0, 0].astype(BF16)

    @pl.when(i < n_used)
    def _():
        _wait_row_gather(h_hbm, xbuf, slot, sem, width, rows)
        xb = _load_row_slabs(xbuf.at[slot], rows, width).astype(BF16)
        hidden = _silu(_dot(xb, wg_bf[...])) * _dot(xb, wu_bf[...])
        _store_row_slabs(y_ref, _dot(hidden.astype(BF16), wd_bf[...]))

    @pl.when(i >= n_used)
    def _():
        y_ref[...] = jnp.zeros_like(y_ref)


def _grouped_experts(h2, row_tok, block_e, n_used, n_blocks, layer, w_gate, w_up, w_down,
                     rows=MOE_ROWS):
    d, hid = w_gate.shape[-2:]
    pitch = _slab_pitch(d)
    tok3 = (row_tok * pitch).reshape(n_blocks, 1, rows)
    smem_rows = lambda shift: pl.BlockSpec(
        (1, 1, rows), lambda i, be, nu: (jnp.minimum(i + shift, n_blocks - 1), 0, 0),
        memory_space=pltpu.SMEM)
    grid_spec = pltpu.PrefetchScalarGridSpec(
        num_scalar_prefetch=2,
        grid=(n_blocks,),
        in_specs=[
            smem_rows(0), smem_rows(1),
            pl.BlockSpec(memory_space=pl.ANY),
            pl.BlockSpec((1, 1, d, hid), lambda i, be, nu: (layer, be[i], 0, 0)),
            pl.BlockSpec((1, 1, d, hid), lambda i, be, nu: (layer, be[i], 0, 0)),
            pl.BlockSpec((1, 1, hid, d), lambda i, be, nu: (layer, be[i], 0, 0)),
        ],
        out_specs=pl.BlockSpec((rows * pitch, LANES), lambda i, be, nu: (i, 0)),
        scratch_shapes=[
            pltpu.VMEM((2, rows * pitch, LANES), F32),
            pltpu.SemaphoreType.DMA((2,)),
            pltpu.VMEM((d, hid), BF16),
            pltpu.VMEM((d, hid), BF16),
            pltpu.VMEM((hid, d), BF16),
        ],
    )
    return pl.pallas_call(
        functools.partial(_expert_kernel, rows=rows),
        grid_spec=grid_spec,
        out_shape=jax.ShapeDtypeStruct((n_blocks * rows * pitch, LANES), F32),
        compiler_params=_params(("arbitrary",)),
        name="grouped_experts",
    )(block_e, n_used, tok3, tok3, h2, w_gate, w_up, w_down)


def _combine_kernel(dest_cur, dest_next, y_hbm, gw_ref, h2_ref, x1_ref, gate_ref, lng_ref, lnb_ref,
                    wg_ref, wu_ref, wd_ref, o_ref, buf, gws, routed, sem, *, tm):
    width = o_ref.shape[1]
    slabs = width // LANES
    pitch = _slab_pitch(width)
    i = pl.program_id(0)
    slot = lax.rem(i, 2)
    n_rows = TOP_K * tm

    @pl.when(i == 0)
    def _():
        for s in range(2):
            buf[s, pl.ds(slabs, n_rows, stride=pitch), :] = jnp.zeros((n_rows, LANES), F32)
        _start_row_gather(y_hbm, dest_cur, buf, 0, sem, width, n_rows)

    @pl.when(i + 1 < pl.num_programs(0))
    def _():
        _start_row_gather(y_hbm, dest_next, buf, 1 - slot, sem, width, n_rows)

    hb = h2_ref[...].astype(BF16)
    hidden = _silu(_dot(hb, wg_ref[...])) * _dot(hb, wu_ref[...])
    ffn = _dot(hidden.astype(BF16), wd_ref[...])

    _wait_row_gather(y_hbm, buf, slot, sem, width, n_rows)
    gw = gw_ref[...]
    for c in range(pitch):
        gws[pl.ds(c, tm, stride=pitch), :] = gw
    gw = gws[...]
    span = tm * pitch
    acc = gw[:, 0:1] * buf[slot, 0:span, :]
    for k in range(1, TOP_K):
        acc = acc + gw[:, k:k + 1] * buf[slot, k * span:(k + 1) * span, :]
    routed[...] = acc
    ffn = ffn + _load_row_slabs(routed, tm, width)
    o_ref[...] = _layer_norm(DEEPNORM_ALPHA * x1_ref[...] + gate_ref[0] * ffn, lng_ref[...], lnb_ref[...])


def _combine(y, dest_of, gates, h2, x1, gate2, ln_g, ln_b, ws_gate, ws_up, ws_down, seq,
             tm=COMBINE_TOKENS):
    n_tok, d = h2.shape
    n_tiles = n_tok // tm
    hid = ws_gate.shape[-1]
    pitch = _slab_pitch(d)
    dest3 = (dest_of * pitch).reshape(n_tiles, tm, TOP_K).transpose(0, 2, 1).reshape(n_tiles, 1, TOP_K * tm)
    smem_rows = lambda shift: pl.BlockSpec(
        (1, 1, TOP_K * tm), lambda i: (jnp.minimum(i + shift, n_tiles - 1), 0, 0),
        memory_space=pltpu.SMEM)
    tile = pl.BlockSpec((tm, d), lambda i: (i, 0))
    row = pl.BlockSpec((1, d), lambda i: (0, 0))
    return pl.pallas_call(
        functools.partial(_combine_kernel, tm=tm),
        grid=(n_tiles,),
        in_specs=[
            smem_rows(0), smem_rows(1),
            pl.BlockSpec(memory_space=pl.ANY),
            pl.BlockSpec((tm, 8), lambda i: (i, 0)),
            tile, tile,
            pl.BlockSpec((1, 1, d), lambda i: (i * tm // seq, 0, 0)),
            row, row,
            pl.BlockSpec((d, hid), lambda i: (0, 0)),
            pl.BlockSpec((d, hid), lambda i: (0, 0)),
            pl.BlockSpec((hid, d), lambda i: (0, 0)),
        ],
        out_specs=tile,
        out_shape=jax.ShapeDtypeStruct((n_tok, d), F32),
        scratch_shapes=[pltpu.VMEM((2, TOP_K * tm * pitch, LANES), F32),
                        pltpu.VMEM((tm * pitch, 8), F32),
                        pltpu.VMEM((tm * pitch, LANES), F32), pltpu.SemaphoreType.DMA((2,))],
        compiler_params=_params(("arbitrary",)),
        name="combine_shared_layernorm",
    )(dest3, dest3, y, gates, h2, x1, gate2, ln_g.reshape(1, d), ln_b.reshape(1, d),
      ws_gate.astype(BF16), ws_up.astype(BF16), ws_down.astype(BF16))


def kernel(x, c, positions, w_ada, b_ada, w_in, b_forget, head_norm_g, w_out, ln1_g, ln1_b, w_router,
           router_bias, w_exp_gate, w_exp_up, w_exp_down, w_sh_gate, w_sh_up, w_sh_down, ln2_g, ln2_b):
    bsz, seq, d = x.shape
    n_tok = bsz * seq
    mod = _ada_modulation(c, w_ada, b_ada)
    cos_t, sin_t = _rope_tables(positions)
    n_a = 3 * (W_SB + W_FOX)
    n_qkv = n_a + 3 * W_DIL
    for l in range(DEPTH):
        shift1, scale1, gate1, shift2, scale2, gate2 = (
            mod[l, :, k * d:(k + 1) * d].reshape(bsz, 1, d) for k in range(6))
        w_a = w_in[l, :, :n_a].astype(BF16)
        w_f = jnp.zeros((d, 2 * LANES), F32).at[:, :N_HEADS_FOX].set(w_in[l, :, n_qkv:])
        w_b = jnp.concatenate([w_in[l, :, n_a:n_qkv], w_f], axis=1).astype(BF16)
        proj_a = _modulated_matmul(x, scale1, shift1, w_a, BF16, tm=1024, tn=n_a // 3)
        proj_b = _modulated_matmul(x, scale1, shift1, w_b, F32, tm=1024, tn=w_b.shape[1] // 2)
        g = head_norm_g[l]
        o_sb = _sb_attention(proj_a, g[:W_SB])
        cum_col, cum_row = _fox_prefix(proj_b, 3 * W_DIL // LANES, b_forget[l])
        cum_row = cum_row[:, :N_HEADS_FOX].reshape(bsz, N_HEADS_FOX // 2, 2, seq)
        o_fx = _fox_attention(proj_a, 3 * W_SB // LANES, cum_col, cum_row, g[W_SB:W_SB + W_FOX])
        o_dl = _dil_attention(proj_b, cos_t, sin_t, g[W_SB + W_FOX:])
        x1, h2, h2_slabs, idx, gates, rank, counts = _output_projection(
            o_sb, o_fx, o_dl, w_out[l], x, gate1, scale2, shift2, ln1_g[l], ln1_b[l], w_router[l],
            router_bias[l])
        h2 = h2.reshape(n_tok, d)
        row_tok, block_e, n_used, dest_of, n_blocks = _dispatch_plan(
            idx.reshape(n_tok, 8)[:, :TOP_K], rank.reshape(n_tok, 8)[:, :TOP_K],
            counts[0, :N_EXPERTS].astype(I32), MOE_ROWS)
        y = _grouped_experts(h2_slabs, row_tok, block_e, n_used, n_blocks, l,
                             w_exp_gate, w_exp_up, w_exp_down)
        x = _combine(y, dest_of, gates.reshape(n_tok, 8), h2, x1.reshape(n_tok, d), gate2, ln2_g[l],
                     ln2_b[l], w_sh_gate[l], w_sh_up[l], w_sh_down[l], seq).reshape(bsz, seq, d)
    return x
```

```python
import functools

import jax
import jax.numpy as jnp
from jax import lax
from jax.experimental import pallas as pl
from jax.experimental.pallas import tpu as pltpu

F32 = jnp.float32
BF16 = jnp.bfloat16
I32 = jnp.int32

D_MODEL = 2048
DEPTH = 2
HEAD_DIM = 64
N_HEADS_SB = 8
N_HEADS_FOX = 12
N_HEADS_DIL = 12
W_SB = N_HEADS_SB * HEAD_DIM
W_FOX = N_HEADS_FOX * HEAD_DIM
W_DIL = N_HEADS_DIL * HEAD_DIM
DILATIONS = (1, 4, 16)
DIL_SPAN = 128
ROPE_THETA = 10000.0
N_EXPERTS = 64
TOP_K = 6
ROUTED_SCALE = 2.5
DEEPNORM_ALPHA = (2 * DEPTH) ** 0.25
LN_EPS = 1e-5
RMS_EPS = 1e-6
QK_SCALE = HEAD_DIM ** -0.5
NEG_INF = float("-inf")

LANES = 128
ATTN_QUERY_BLOCK = 512
ATTN_KEY_BLOCK = 256
MOE_ROWS = 256
COMBINE_TOKENS = 128
VMEM_LIMIT = 56 * 1024 * 1024


def _params(semantics, vmem=VMEM_LIMIT):
    return pltpu.CompilerParams(dimension_semantics=semantics, vmem_limit_bytes=vmem)


def _dot(a, b):
    return jnp.dot(a, b, preferred_element_type=F32)


def _dot_nt(a, b):
    return lax.dot_general(a, b, (((1,), (1,)), ((), ())), preferred_element_type=F32)


def _split2(a):
    hi = a.astype(BF16)
    lo = (a - hi.astype(F32)).astype(BF16)
    return hi, lo


def _split3(a):
    hi = a.astype(BF16)
    r1 = a - hi.astype(F32)
    mid = r1.astype(BF16)
    lo = (r1 - mid.astype(F32)).astype(BF16)
    return hi, mid, lo


def _dot_f32(a, b):
    ah, al = _split2(a)
    bh, bl = _split2(b)
    return _dot(ah, bh) + _dot(ah, bl) + _dot(al, bh)


def _softplus(z):
    return jnp.maximum(z, 0.0) + jnp.log1p(jnp.exp(-jnp.abs(z)))


def _silu(z):
    return z * jax.nn.sigmoid(z)


def _layer_norm(r, g, b):
    mu = jnp.mean(r, axis=-1, keepdims=True)
    d = r - mu
    var = jnp.mean(d * d, axis=-1, keepdims=True)
    return d * lax.rsqrt(var + LN_EPS) * g + b


def _slab_pitch(width):
    return width // LANES + 1


def _store_row_slabs(dst, value):
    n, width = value.shape
    slabs = width // LANES
    pitch = _slab_pitch(width)
    for c in range(slabs):
        dst[pl.ds(c, n, stride=pitch), :] = value[:, c * LANES:(c + 1) * LANES]
    dst[pl.ds(slabs, n, stride=pitch), :] = jnp.zeros((n, LANES), dst.dtype)


def _load_row_slabs(src, n, width):
    pitch = _slab_pitch(width)
    return jnp.concatenate([src[pl.ds(c, n, stride=pitch), :] for c in range(width // LANES)], axis=1)


def _pair_rms_norm(o, g, lane):
    left = lane < HEAD_DIM
    sq = o * o
    s0 = jnp.sum(jnp.where(left, sq, 0.0), axis=-1, keepdims=True)
    s1 = jnp.sum(jnp.where(left, 0.0, sq), axis=-1, keepdims=True)
    ms = jnp.where(left, s0, s1) * (1.0 / HEAD_DIM)
    return o * lax.rsqrt(ms + RMS_EPS) * g


def _ada_kernel(c_ref, w_ref, b_ref, o_ref):
    o_ref[0] = _dot_f32(_silu(c_ref[...]), w_ref[0]) + b_ref[0]


def _ada_modulation(c, w_ada, b_ada, tn=1024):
    depth, d, n = w_ada.shape
    bsz = c.shape[0]
    return pl.pallas_call(
        _ada_kernel,
        grid=(depth, n // tn),
        in_specs=[
            pl.BlockSpec((bsz, d), lambda l, j: (0, 0)),
            pl.BlockSpec((1, d, tn), lambda l, j: (l, 0, j)),
            pl.BlockSpec((1, 1, tn), lambda l, j: (l, 0, j)),
        ],
        out_specs=pl.BlockSpec((1, bsz, tn), lambda l, j: (l, 0, j)),
        out_shape=jax.ShapeDtypeStruct((depth, bsz, n), F32),
        compiler_params=_params(("arbitrary", "arbitrary")),
        name="ada_modulation",
    )(c, w_ada, b_ada.reshape(depth, 1, n))


def _rope_kernel(pos_ref, invf_ref, sign_ref, cos_ref, sin_ref):
    ang = pos_ref[0].astype(F32) * invf_ref[...]
    cos_ref[0] = jnp.cos(ang)
    sin_ref[0] = jnp.sin(ang) * sign_ref[...]


def _rope_tables(positions, ts=512):
    bsz, seq = positions.shape
    half = HEAD_DIM // 2
    inv_freq = ROPE_THETA ** (-jnp.arange(half, dtype=F32) / half)
    inv_lane = jnp.tile(inv_freq, LANES // half).reshape(1, LANES)
    sign = jnp.tile(jnp.concatenate([-jnp.ones((half,), F32), jnp.ones((half,), F32)]),
                    LANES // HEAD_DIM).reshape(1, LANES)
    tab = jax.ShapeDtypeStruct((bsz, seq, LANES), F32)
    return pl.pallas_call(
        _rope_kernel,
        grid=(bsz, seq // ts),
        in_specs=[
            pl.BlockSpec((1, ts, 1), lambda b, i: (b, i, 0)),
            pl.BlockSpec((1, LANES), lambda b, i: (0, 0)),
            pl.BlockSpec((1, LANES), lambda b, i: (0, 0)),
        ],
        out_specs=[pl.BlockSpec((1, ts, LANES), lambda b, i: (b, i, 0))] * 2,
        out_shape=[tab, tab],
        compiler_params=_params(("arbitrary", "arbitrary")),
        name="rope_tables",
    )(positions.reshape(bsz, seq, 1), inv_lane, sign)


def _modmm_kernel(x_ref, sc_ref, sh_ref, w_ref, o_ref):
    h = x_ref[0] * (1.0 + sc_ref[0]) + sh_ref[0]
    o_ref[0] = _dot(h.astype(BF16), w_ref[...]).astype(o_ref.dtype)


def _modulated_matmul(x, scale, shift, w, out_dtype, tm, tn):
    bsz, seq, d = x.shape
    n = w.shape[1]
    return pl.pallas_call(
        _modmm_kernel,
        grid=(n // tn, bsz, seq // tm),
        in_specs=[
            pl.BlockSpec((1, tm, d), lambda j, b, i: (b, i, 0)),
            pl.BlockSpec((1, 1, d), lambda j, b, i: (b, 0, 0)),
            pl.BlockSpec((1, 1, d), lambda j, b, i: (b, 0, 0)),
            pl.BlockSpec((d, tn), lambda j, b, i: (0, j)),
        ],
        out_specs=pl.BlockSpec((1, tm, tn), lambda j, b, i: (b, i, j)),
        out_shape=jax.ShapeDtypeStruct((bsz, seq, n), out_dtype),
        compiler_params=_params(("arbitrary", "arbitrary", "arbitrary")),
        name="modulated_projection",
    )(x, scale, shift, w)


def _stack_pair_queries(q, lane):
    zero = jnp.zeros_like(q)
    return jnp.concatenate([jnp.where(lane < HEAD_DIM, q, zero), jnp.where(lane < HEAD_DIM, zero, q)], axis=0)


def _sb_kernel(q_ref, k_ref, v_ref, g_ref, o_ref, *, tq, tk):
    qi = pl.program_id(2)
    sub = tq // tk
    lane = lax.broadcasted_iota(I32, (1, LANES), 1)
    row = lax.broadcasted_iota(I32, (2 * tq, tk), 0) & (tq - 1)
    col = lax.broadcasted_iota(I32, (2 * tq, tk), 1)
    r2 = lax.broadcasted_iota(I32, (tk, tk), 0)
    c2 = lax.broadcasted_iota(I32, (tk, tk), 1)
    later_keys = jnp.where(r2 > c2, 1.0, 0.0).astype(BF16)
    later_keys = jnp.concatenate([later_keys, later_keys], axis=0)
    q2 = _stack_pair_queries(q_ref[0] * QK_SCALE, lane)

    def block(kj, carry, diag):
        o, tail = carry
        start = pl.multiple_of(kj * tk, tk)
        kb = k_ref[0, pl.ds(start, tk), :]
        vb = v_ref[0, pl.ds(start, tk), :]
        z = _dot_nt(q2, kb)
        drop = jnp.log(1.0 + jnp.exp(-jnp.abs(z))) + jnp.maximum(z, 0.0)
        if diag is not None:
            past = col + diag * tk < row
            drop = jnp.where(past, drop, 0.0)
        hi, lo = _split2(drop)
        later = _dot(jnp.concatenate([hi, lo], axis=1), later_keys) + tail
        w = jnp.exp(z - drop - later)
        if diag is not None:
            w = jnp.where(past, w, 0.0)
        o = o + _dot(w.astype(BF16), vb)
        tail = tail + jnp.sum(drop, axis=-1, keepdims=True)
        return o, tail

    def earlier_blocks(it, carry):
        base = (qi - 1 - it) * sub
        for j in reversed(range(sub)):
            carry = block(base + j, carry, None)
        return carry

    carry = (jnp.zeros((2 * tq, LANES), F32), jnp.zeros((2 * tq, 1), F32))
    for diag in reversed(range(sub)):
        carry = block(qi * sub + diag, carry, diag)
    carry = lax.fori_loop(0, qi, earlier_blocks, carry)
    o = jnp.where(lane < HEAD_DIM, carry[0][:tq], carry[0][tq:])
    o_ref[0] = _pair_rms_norm(o, g_ref[...], lane).astype(o_ref.dtype)


def _sb_attention(proj, gain, blk=ATTN_QUERY_BLOCK):
    bsz, seq, _ = proj.shape
    pairs = W_SB // LANES
    return pl.pallas_call(
        functools.partial(_sb_kernel, tq=blk, tk=ATTN_KEY_BLOCK),
        grid=(bsz, pairs, seq // blk),
        in_specs=[
            pl.BlockSpec((1, blk, LANES), lambda b, p, i: (b, i, p)),
            pl.BlockSpec((1, seq, LANES), lambda b, p, i: (b, 0, pairs + p)),
            pl.BlockSpec((1, seq, LANES), lambda b, p, i: (b, 0, 2 * pairs + p)),
            pl.BlockSpec((1, LANES), lambda b, p, i: (0, p)),
        ],
        out_specs=pl.BlockSpec((1, blk, LANES), lambda b, p, i: (b, i, p)),
        out_shape=jax.ShapeDtypeStruct((bsz, seq, W_SB), BF16),
        compiler_params=_params(("arbitrary", "arbitrary", "arbitrary")),
        name="stick_breaking_attention",
    )(proj, proj, proj, gain.reshape(1, W_SB))


def _fox_prep_kernel(f_ref, b_ref, col_ref, row_ref, *, blk):
    seq = f_ref.shape[1]
    x = f_ref[0] + b_ref[...]
    log_f = -_softplus(-x)
    r = lax.broadcasted_iota(I32, (blk, blk), 0)
    c = lax.broadcasted_iota(I32, (blk, blk), 1)
    upto = jnp.where(c <= r, 1.0, 0.0).astype(BF16)
    carry = jnp.zeros((1, LANES), F32)
    parts = []
    for i in range(seq // blk):
        hi, mid, lo = _split3(log_f[i * blk:(i + 1) * blk])
        cs = _dot(upto, hi) + _dot(upto, mid) + _dot(upto, lo) + carry
        parts.append(cs)
        carry = cs[blk - 1:blk, :]
    cum = jnp.concatenate(parts, axis=0)
    col_ref[0] = cum
    row_ref[0] = cum.T


def _fox_prefix(f_logits, f_block, b_forget):
    bsz, seq, _ = f_logits.shape
    bias = jnp.zeros((1, LANES), F32).at[0, :N_HEADS_FOX].set(b_forget)
    return pl.pallas_call(
        functools.partial(_fox_prep_kernel, blk=256),
        grid=(bsz,),
        in_specs=[
            pl.BlockSpec((1, seq, LANES), lambda b: (b, 0, f_block)),
            pl.BlockSpec((1, LANES), lambda b: (0, 0)),
        ],
        out_specs=[
            pl.BlockSpec((1, seq, LANES), lambda b: (b, 0, 0)),
            pl.BlockSpec((1, LANES, seq), lambda b: (b, 0, 0)),
        ],
        out_shape=[jax.ShapeDtypeStruct((bsz, seq, LANES), F32),
                   jax.ShapeDtypeStruct((bsz, LANES, seq), F32)],
        compiler_params=_params(("arbitrary",)),
        name="forget_prefix",
    )(f_logits, bias)


def _fox_kernel(q_ref, k_ref, v_ref, cq_ref, ck_ref, g_ref, o_ref, *, tq, tk):
    pair = pl.program_id(1)
    qi = pl.program_id(2)
    sub = tq // tk
    lane = lax.broadcasted_iota(I32, (1, LANES), 1)
    row = lax.broadcasted_iota(I32, (2 * tq, tk), 0) & (tq - 1)
    col = lax.broadcasted_iota(I32, (2 * tq, tk), 1)
    q2 = _stack_pair_queries(q_ref[0] * QK_SCALE, lane)
    cum_q_all = cq_ref[0]
    cum_q = [jnp.sum(jnp.where(lane == 2 * pair + h, cum_q_all, 0.0), axis=-1, keepdims=True)
             for h in range(2)]

    def block(kj, size, carry, diag):
        m, l, o = carry
        start = pl.multiple_of(kj * size, size)
        kb = k_ref[0, pl.ds(start, size), :]
        vb = v_ref[0, pl.ds(start, size), :]
        qk = _dot_nt(q2, kb)
        z = jnp.concatenate(
            [qk[h * tq:(h + 1) * tq] + (cum_q[h] - ck_ref[0, 0, h:h + 1, pl.ds(start, size)])
             for h in range(2)], axis=0)
        if diag is not None:
            z = jnp.where(col + diag * tk <= row, z, NEG_INF)
        m_new = jnp.maximum(m, jnp.max(z, axis=-1, keepdims=True))
        p = jnp.exp(z - m_new)
        a = jnp.exp(m - m_new)
        l = a * l + jnp.sum(p, axis=-1, keepdims=True)
        o = a * o + _dot(p.astype(BF16), vb)
        return m_new, l, o

    carry = (jnp.full((2 * tq, 1), NEG_INF, F32), jnp.zeros((2 * tq, 1), F32),
             jnp.zeros((2 * tq, LANES), F32))
    for diag in range(sub):
        carry = block(qi * sub + diag, tk, carry, diag)
    carry = lax.fori_loop(0, qi, lambda kj, c: block(kj, tq, c, None), carry)
    o = carry[2] / carry[1]
    o = jnp.where(lane < HEAD_DIM, o[:tq], o[tq:])
    o_ref[0] = _pair_rms_norm(o, g_ref[...], lane).astype(o_ref.dtype)


def _fox_attention(proj, col_base, cum_col, cum_row, gain, blk=ATTN_QUERY_BLOCK):
    bsz, seq, _ = proj.shape
    pairs = W_FOX // LANES
    return pl.pallas_call(
        functools.partial(_fox_kernel, tq=blk, tk=ATTN_KEY_BLOCK),
        grid=(bsz, pairs, seq // blk),
        in_specs=[
            pl.BlockSpec((1, blk, LANES), lambda b, p, i: (b, i, col_base + p)),
            pl.BlockSpec((1, seq, LANES), lambda b, p, i: (b, 0, col_base + pairs + p)),
            pl.BlockSpec((1, seq, LANES), lambda b, p, i: (b, 0, col_base + 2 * pairs + p)),
            pl.BlockSpec((1, blk, LANES), lambda b, p, i: (b, i, 0)),
            pl.BlockSpec((1, 1, 2, seq), lambda b, p, i: (b, p, 0, 0)),
            pl.BlockSpec((1, LANES), lambda b, p, i: (0, p)),
        ],
        out_specs=pl.BlockSpec((1, blk, LANES), lambda b, p, i: (b, i, p)),
        out_shape=jax.ShapeDtypeStruct((bsz, seq, W_FOX), BF16),
        compiler_params=_params(("arbitrary", "arbitrary", "arbitrary")),
        name="forgetting_attention",
    )(proj, proj, proj, cum_col, cum_row, gain.reshape(1, W_FOX))


def _dil_kernel(q_ref, k_ref, v_ref, cos_ref, sin_ref, g_ref, o_ref,
                qs, ks, vs, og, lg, *, seq, pad):
    lane = lax.broadcasted_iota(I32, (1, LANES), 1)
    first_half = (lane & (HEAD_DIM - 1)) < HEAD_DIM // 2
    cos = cos_ref[0]
    sin = sin_ref[0]

    def rope(t):
        partner = jnp.where(first_half, pltpu.roll(t, LANES - HEAD_DIM // 2, 1),
                            pltpu.roll(t, HEAD_DIM // 2, 1))
        return t * cos + partner * sin

    qs[...] = rope(q_ref[0]) * QK_SCALE
    ks[0:pad, :] = jnp.zeros((pad, LANES), F32)
    vs[0:pad, :] = jnp.zeros((pad, LANES), F32)
    ks[pad:pad + seq, :] = rope(k_ref[0])
    vs[pad:pad + seq, :] = v_ref[0]

    span = DIL_SPAN
    qpos = lax.broadcasted_iota(I32, (2 * span, 2 * span), 0) & (span - 1)
    kpos = lax.broadcasted_iota(I32, (2 * span, 2 * span), 1)
    band = (kpos >= qpos) & (kpos <= qpos + span)

    for gi, dil in enumerate(DILATIONS):
        nblk = seq // dil // span

        def body(idx, _, gi=gi, dil=dil):
            r = lax.rem(idx, dil)
            n = lax.div(idx, dil)
            q0 = r + n * (span * dil)
            k0 = pad + q0 - span * dil
            if dil == 1:
                q_rows = pl.ds(q0, span)
                k_rows = pl.ds(k0, 2 * span)
            else:
                q_rows = pl.ds(q0, span, stride=dil)
                k_rows = pl.ds(k0, 2 * span, stride=dil)
            q2 = _stack_pair_queries(qs[q_rows, :].astype(BF16), lane)
            kb = ks[k_rows, :].astype(BF16)
            vb = vs[k_rows, :].astype(BF16)
            valid = band & (kpos + (n - 1) * span >= 0)
            z = jnp.where(valid, _dot_nt(q2, kb), NEG_INF)
            m = jnp.max(z, axis=-1, keepdims=True)
            p = jnp.exp(z - m)
            den = jnp.sum(p, axis=-1, keepdims=True)
            out = _dot(p.astype(BF16), vb) / den
            lse = m + jnp.log(den)
            og[gi, q_rows, :] = jnp.where(lane < HEAD_DIM, out[:span], out[span:])
            lg[gi, q_rows, :] = jnp.where(lane < HEAD_DIM, lse[:span], lse[span:])
            return 0

        lax.fori_loop(0, dil * nblk, body, 0, unroll=8)

    lse = lg[...]
    top = jnp.max(lse, axis=0)
    wts = jnp.exp(lse - top[None])
    o = jnp.sum(wts * og[...], axis=0) / jnp.sum(wts, axis=0)
    o_ref[0] = _pair_rms_norm(o, g_ref[...], lane).astype(o_ref.dtype)


def _dil_attention(proj, cos_t, sin_t, gain):
    bsz, seq, _ = proj.shape
    pairs = W_DIL // LANES
    pad = DIL_SPAN * max(DILATIONS)
    assert all(seq % (d * DIL_SPAN) == 0 for d in DILATIONS)
    return pl.pallas_call(
        functools.partial(_dil_kernel, seq=seq, pad=pad),
        grid=(bsz, pairs),
        in_specs=[
            pl.BlockSpec((1, seq, LANES), lambda b, p: (b, 0, p)),
            pl.BlockSpec((1, seq, LANES), lambda b, p: (b, 0, pairs + p)),
            pl.BlockSpec((1, seq, LANES), lambda b, p: (b, 0, 2 * pairs + p)),
            pl.BlockSpec((1, seq, LANES), lambda b, p: (b, 0, 0)),
            pl.BlockSpec((1, seq, LANES), lambda b, p: (b, 0, 0)),
            pl.BlockSpec((1, LANES), lambda b, p: (0, p)),
        ],
        out_specs=pl.BlockSpec((1, seq, LANES), lambda b, p: (b, 0, p)),
        out_shape=jax.ShapeDtypeStruct((bsz, seq, W_DIL), BF16),
        scratch_shapes=[
            pltpu.VMEM((seq, LANES), F32),
            pltpu.VMEM((pad + seq, LANES), F32),
            pltpu.VMEM((pad + seq, LANES), F32),
            pltpu.VMEM((len(DILATIONS), seq, LANES), F32),
            pltpu.VMEM((len(DILATIONS), seq, LANES), F32),
        ],
        compiler_params=_params(("arbitrary", "arbitrary")),
        name="dilated_attention",
    )(proj, proj, proj, cos_t, sin_t, gain.reshape(1, W_DIL))


def _oproj_kernel(osb_ref, ofx_ref, odl_ref, w1_ref, w2_ref, w3_ref, x_ref, gate_ref, sc_ref, sh_ref,
                  lng_ref, lnb_ref, wr_ref, rb_ref, x1_ref, h2_ref, h2s_ref, idx_ref, gw_ref, rank_ref,
                  cnt_ref, seen_ref):
    @pl.when((pl.program_id(0) == 0) & (pl.program_id(1) == 0))
    def _():
        seen_ref[...] = jnp.zeros_like(seen_ref)

    mix = (_dot(osb_ref[0], w1_ref[...]) + _dot(ofx_ref[0], w2_ref[...])
           + _dot(odl_ref[0], w3_ref[...]))
    x1 = _layer_norm(DEEPNORM_ALPHA * x_ref[0] + gate_ref[0] * mix, lng_ref[...], lnb_ref[...])
    x1_ref[0] = x1
    h2 = x1 * (1.0 + sc_ref[0]) + sh_ref[0]
    h2_ref[0] = h2
    _store_row_slabs(h2s_ref, h2)

    scores = jax.nn.sigmoid(_dot_f32(h2, wr_ref[...]))
    tm = scores.shape[0]
    lane = lax.broadcasted_iota(I32, (1, LANES), 1)
    lane_f = lane.astype(F32)
    sel = jnp.where(lane < N_EXPERTS, scores + rb_ref[...], NEG_INF)
    idx_acc = jnp.zeros((tm, LANES), I32)
    gate_acc = jnp.zeros((tm, LANES), F32)
    total = jnp.zeros((tm, 1), F32)
    onehot = jnp.zeros((tm, LANES), F32)
    picks = []
    for k in range(TOP_K):
        best = jnp.max(sel, axis=-1, keepdims=True)
        pick = jnp.min(jnp.where(sel == best, lane_f, float(LANES)), axis=-1, keepdims=True)
        chosen = lane_f == pick
        s = jnp.sum(jnp.where(chosen, scores, 0.0), axis=-1, keepdims=True)
        sel = jnp.where(chosen, NEG_INF, sel)
        idx_acc = jnp.where(lane == k, pick.astype(I32), idx_acc)
        gate_acc = jnp.where(lane == k, s, gate_acc)
        onehot = jnp.where(chosen, 1.0, onehot)
        total = total + s
        picks.append(pick)
    idx_ref[0] = idx_acc[:, :8]
    gw_ref[0] = (gate_acc / total * ROUTED_SCALE)[:, :8]

    r = lax.broadcasted_iota(I32, (tm, tm), 0)
    c = lax.broadcasted_iota(I32, (tm, tm), 1)
    earlier = jnp.where(c < r, 1.0, 0.0).astype(BF16)
    rank = _dot(earlier, onehot.astype(BF16)) + seen_ref[...]
    rank_acc = jnp.zeros((tm, LANES), F32)
    for k in range(TOP_K):
        rk = jnp.sum(jnp.where(lane_f == picks[k], rank, 0.0), axis=-1, keepdims=True)
        rank_acc = jnp.where(lane == k, rk, rank_acc)
    rank_ref[0] = rank_acc[:, :8].astype(I32)
    seen_ref[...] = seen_ref[...] + jnp.sum(onehot, axis=0, keepdims=True)
    cnt_ref[...] = seen_ref[...]


def _output_projection(o_sb, o_fx, o_dl, w_out, x, gate1, scale2, shift2, ln_g, ln_b, w_router,
                       router_bias, tm=256):
    bsz, seq, d = x.shape
    pitch = _slab_pitch(d)
    w1 = w_out[:W_SB].astype(BF16)
    w2 = w_out[W_SB:W_SB + W_FOX].astype(BF16)
    w3 = w_out[W_SB + W_FOX:].astype(BF16)
    wr = jnp.zeros((d, LANES), F32).at[:, :N_EXPERTS].set(w_router.astype(F32))
    rb = jnp.zeros((1, LANES), F32).at[0, :N_EXPERTS].set(router_bias.astype(F32))
    tile = lambda w: pl.BlockSpec((1, tm, w), lambda b, i: (b, i, 0))
    whole = lambda a: pl.BlockSpec(a.shape, lambda b, i: (0,) * a.ndim)
    per_batch = pl.BlockSpec((1, 1, d), lambda b, i: (b, 0, 0))
    return pl.pallas_call(
        _oproj_kernel,
        grid=(bsz, seq // tm),
        in_specs=[tile(W_SB), tile(W_FOX), tile(W_DIL), whole(w1), whole(w2), whole(w3), tile(d),
                  per_batch, per_batch, per_batch,
                  pl.BlockSpec((1, d), lambda b, i: (0, 0)), pl.BlockSpec((1, d), lambda b, i: (0, 0)),
                  whole(wr), whole(rb)],
        out_specs=[tile(d), tile(d),
                   pl.BlockSpec((tm * pitch, LANES), lambda b, i: (b * (seq // tm) + i, 0)),
                   tile(8), tile(8), tile(8),
                   pl.BlockSpec((1, LANES), lambda b, i: (0, 0))],
        out_shape=[jax.ShapeDtypeStruct((bsz, seq, d), F32), jax.ShapeDtypeStruct((bsz, seq, d), F32),
                   jax.ShapeDtypeStruct((bsz * seq * pitch, LANES), F32),
                   jax.ShapeDtypeStruct((bsz, seq, 8), I32), jax.ShapeDtypeStruct((bsz, seq, 8), F32),
                   jax.ShapeDtypeStruct((bsz, seq, 8), I32), jax.ShapeDtypeStruct((1, LANES), F32)],
        scratch_shapes=[pltpu.VMEM((1, LANES), F32)],
        compiler_params=_params(("arbitrary", "arbitrary")),
        name="output_projection_router",
    )(o_sb, o_fx, o_dl, w1, w2, w3, x, gate1, scale2, shift2, ln_g.reshape(1, d), ln_b.reshape(1, d),
      wr, rb)


def _dispatch_plan(idx, rank, counts, rows_per_block):
    n_tok = idx.shape[0]
    n_assign = n_tok * TOP_K
    padded = (counts + rows_per_block - 1) // rows_per_block * rows_per_block
    pad_end = jnp.cumsum(padded)
    pad_start = pad_end - padded
    dest_of = (jnp.take(pad_start, idx) + rank).astype(I32)
    n_blocks = -(-(n_assign + N_EXPERTS * (rows_per_block - 1)) // rows_per_block)
    rows = n_blocks * rows_per_block
    block_first_row = jnp.arange(n_blocks, dtype=I32) * rows_per_block
    block_e = jnp.sum((pad_end[None, :] <= block_first_row[:, None]).astype(I32), axis=1)
    block_e = jnp.minimum(block_e, N_EXPERTS - 1).astype(I32)
    n_used = (pad_end[-1] // rows_per_block).astype(I32).reshape(1)
    order = jnp.argsort(idx.reshape(-1), stable=True).astype(I32)
    first = jnp.cumsum(counts) - counts
    local = ((block_first_row - jnp.take(pad_start, block_e))[:, None]
             + jnp.arange(rows_per_block, dtype=I32)[None, :])
    src = jnp.clip(jnp.take(first, block_e)[:, None] + local, 0, n_assign - 1)
    row_tok = jnp.where(local < jnp.take(counts, block_e)[:, None], jnp.take(order, src) // TOP_K, 0)
    return row_tok.astype(I32).reshape(rows), block_e, n_used, dest_of, n_blocks


def _row_copy(src_hbm, first, dst, slot, r, sem, width):
    slabs = width // LANES
    return pltpu.make_async_copy(src_hbm.at[pl.ds(first, slabs), :],
                                 dst.at[slot, pl.ds(r * _slab_pitch(width), slabs), :], sem.at[slot])


def _start_row_gather(src_hbm, first_ref, dst, slot, sem, width, n):
    def body(r, _):
        _row_copy(src_hbm, first_ref[0, 0, r], dst, slot, r, sem, width).start()
        return 0
    lax.fori_loop(0, n, body, 0, unroll=8)


def _wait_row_gather(src_hbm, dst, slot, sem, width, n):
    def body(r, _):
        _row_copy(src_hbm, 0, dst, slot, r, sem, width).wait()
        return 0
    lax.fori_loop(0, n, body, 0, unroll=8)


def _expert_kernel(be_ref, nu_ref, tok_cur, tok_next, h_hbm, wg_ref, wu_ref, wd_ref, y_ref,
                   xbuf, sem, wg_bf, wu_bf, wd_bf, *, rows):
    width = wg_bf.shape[0]
    i = pl.program_id(0)
    slot = lax.rem(i, 2)
    n_used = nu_ref[0]

    @pl.when(i == 0)
    def _():
        _start_row_gather(h_hbm, tok_cur, xbuf, 0, sem, width, rows)

    @pl.when(i + 1 < n_used)
    def _():
        _start_row_gather(h_hbm, tok_next, xbuf, 1 - slot, sem, width, rows)

    new_expert = (i == 0) | (be_ref[i] != be_ref[jnp.maximum(i - 1, 0)])

    @pl.when(new_expert & (i < n_used))
    def _():
        wg_bf[...] = wg_ref[0, 0].astype(BF16)
        wu_bf[...] = wu_ref[0, 0].astype(BF16)
        wd_bf[...] = wd_ref[0, 0].astype(BF16)

    @pl.when(i < n_used)
    def _():
        _wait_row_gather(h_hbm, xbuf, slot, sem, width, rows)
        xb = _load_row_slabs(xbuf.at[slot], rows, width).astype(BF16)
        hidden = _silu(_dot(xb, wg_bf[...])) * _dot(xb, wu_bf[...])
        _store_row_slabs(y_ref, _dot(hidden.astype(BF16), wd_bf[...]))

    @pl.when(i >= n_used)
    def _():
        y_ref[...] = jnp.zeros_like(y_ref)


def _grouped_experts(h2, row_tok, block_e, n_used, n_blocks, layer, w_gate, w_up, w_down,
                     rows=MOE_ROWS):
    d, hid = w_gate.shape[-2:]
    pitch = _slab_pitch(d)
    tok3 = (row_tok * pitch).reshape(n_blocks, 1, rows)
    smem_rows = lambda shift: pl.BlockSpec(
        (1, 1, rows), lambda i, be, nu: (jnp.minimum(i + shift, n_blocks - 1), 0, 0),
        memory_space=pltpu.SMEM)
    grid_spec = pltpu.PrefetchScalarGridSpec(
        num_scalar_prefetch=2,
        grid=(n_blocks,),
        in_specs=[
            smem_rows(0), smem_rows(1),
            pl.BlockSpec(memory_space=pl.ANY),
            pl.BlockSpec((1, 1, d, hid), lambda i, be, nu: (layer, be[i], 0, 0)),
            pl.BlockSpec((1, 1, d, hid), lambda i, be, nu: (layer, be[i], 0, 0)),
            pl.BlockSpec((1, 1, hid, d), lambda i, be, nu: (layer, be[i], 0, 0)),
        ],
        out_specs=pl.BlockSpec((rows * pitch, LANES), lambda i, be, nu: (i, 0)),
        scratch_shapes=[
            pltpu.VMEM((2, rows * pitch, LANES), F32),
            pltpu.SemaphoreType.DMA((2,)),
            pltpu.VMEM((d, hid), BF16),
            pltpu.VMEM((d, hid), BF16),
            pltpu.VMEM((hid, d), BF16),
        ],
    )
    return pl.pallas_call(
        functools.partial(_expert_kernel, rows=rows),
        grid_spec=grid_spec,
        out_shape=jax.ShapeDtypeStruct((n_blocks * rows * pitch, LANES), F32),
        compiler_params=_params(("arbitrary",)),
        name="grouped_experts",
    )(block_e, n_used, tok3, tok3, h2, w_gate, w_up, w_down)


def _combine_kernel(dest_cur, dest_next, y_hbm, gw_ref, h2_ref, x1_ref, gate_ref, lng_ref, lnb_ref,
                    wg_ref, wu_ref, wd_ref, o_ref, buf, gws, routed, sem, *, tm):
    width = o_ref.shape[1]
    slabs = width // LANES
    pitch = _slab_pitch(width)
    i = pl.program_id(0)
    slot = lax.rem(i, 2)
    n_rows = TOP_K * tm

    @pl.when(i == 0)
    def _():
        for s in range(2):
            buf[s, pl.ds(slabs, n_rows, stride=pitch), :] = jnp.zeros((n_rows, LANES), F32)
        _start_row_gather(y_hbm, dest_cur, buf, 0, sem, width, n_rows)

    @pl.when(i + 1 < pl.num_programs(0))
    def _():
        _start_row_gather(y_hbm, dest_next, buf, 1 - slot, sem, width, n_rows)

    hb = h2_ref[...].astype(BF16)
    hidden = _silu(_dot(hb, wg_ref[...])) * _dot(hb, wu_ref[...])
    ffn = _dot(hidden.astype(BF16), wd_ref[...])

    _wait_row_gather(y_hbm, buf, slot, sem, width, n_rows)
    gw = gw_ref[...]
    for c in range(pitch):
        gws[pl.ds(c, tm, stride=pitch), :] = gw
    gw = gws[...]
    span = tm * pitch
    acc = gw[:, 0:1] * buf[slot, 0:span, :]
    for k in range(1, TOP_K):
        acc = acc + gw[:, k:k + 1] * buf[slot, k * span:(k + 1) * span, :]
    routed[...] = acc
    ffn = ffn + _load_row_slabs(routed, tm, width)
    o_ref[...] = _layer_norm(DEEPNORM_ALPHA * x1_ref[...] + gate_ref[0] * ffn, lng_ref[...], lnb_ref[...])


def _combine(y, dest_of, gates, h2, x1, gate2, ln_g, ln_b, ws_gate, ws_up, ws_down, seq,
             tm=COMBINE_TOKENS):
    n_tok, d = h2.shape
    n_tiles = n_tok // tm
    hid = ws_gate.shape[-1]
    pitch = _slab_pitch(d)
    dest3 = (dest_of * pitch).reshape(n_tiles, tm, TOP_K).transpose(0, 2, 1).reshape(n_tiles, 1, TOP_K * tm)
    smem_rows = lambda shift: pl.BlockSpec(
        (1, 1, TOP_K * tm), lambda i: (jnp.minimum(i + shift, n_tiles - 1), 0, 0),
        memory_space=pltpu.SMEM)
    tile = pl.BlockSpec((tm, d), lambda i: (i, 0))
    row = pl.BlockSpec((1, d), lambda i: (0, 0))
    return pl.pallas_call(
        functools.partial(_combine_kernel, tm=tm),
        grid=(n_tiles,),
        in_specs=[
            smem_rows(0), smem_rows(1),
            pl.BlockSpec(memory_space=pl.ANY),
            pl.BlockSpec((tm, 8), lambda i: (i, 0)),
            tile, tile,
            pl.BlockSpec((1, 1, d), lambda i: (i * tm // seq, 0, 0)),
            row, row,
            pl.BlockSpec((d, hid), lambda i: (0, 0)),
            pl.BlockSpec((d, hid), lambda i: (0, 0)),
            pl.BlockSpec((hid, d), lambda i: (0, 0)),
        ],
        out_specs=tile,
        out_shape=jax.ShapeDtypeStruct((n_tok, d), F32),
        scratch_shapes=[pltpu.VMEM((2, TOP_K * tm * pitch, LANES), F32),
                        pltpu.VMEM((tm * pitch, 8), F32),
                        pltpu.VMEM((tm * pitch, LANES), F32), pltpu.SemaphoreType.DMA((2,))],
        compiler_params=_params(("arbitrary",)),
        name="combine_shared_layernorm",
    )(dest3, dest3, y, gates, h2, x1, gate2, ln_g.reshape(1, d), ln_b.reshape(1, d),
      ws_gate.astype(BF16), ws_up.astype(BF16), ws_down.astype(BF16))


def kernel(x, c, positions, w_ada, b_ada, w_in, b_forget, head_norm_g, w_out, ln1_g, ln1_b, w_router,
           router_bias, w_exp_gate, w_exp_up, w_exp_down, w_sh_gate, w_sh_up, w_sh_down, ln2_g, ln2_b):
    bsz, seq, d = x.shape
    n_tok = bsz * seq
    mod = _ada_modulation(c, w_ada, b_ada)
    cos_t, sin_t = _rope_tables(positions)
    n_a = 3 * (W_SB + W_FOX)
    n_qkv = n_a + 3 * W_DIL
    for l in range(DEPTH):
        shift1, scale1, gate1, shift2, scale2, gate2 = (
            mod[l, :, k * d:(k + 1) * d].reshape(bsz, 1, d) for k in range(6))
        w_a = w_in[l, :, :n_a].astype(BF16)
        w_f = jnp.zeros((d, 2 * LANES), F32).at[:, :N_HEADS_FOX].set(w_in[l, :, n_qkv:])
        w_b = jnp.concatenate([w_in[l, :, n_a:n_qkv], w_f], axis=1).astype(BF16)
        proj_a = _modulated_matmul(x, scale1, shift1, w_a, BF16, tm=1024, tn=n_a // 3)
        proj_b = _modulated_matmul(x, scale1, shift1, w_b, F32, tm=1024, tn=w_b.shape[1] // 2)
        g = head_norm_g[l]
        o_sb = _sb_attention(proj_a, g[:W_SB])
        cum_col, cum_row = _fox_prefix(proj_b, 3 * W_DIL // LANES, b_forget[l])
        cum_row = cum_row[:, :N_HEADS_FOX].reshape(bsz, N_HEADS_FOX // 2, 2, seq)
        o_fx = _fox_attention(proj_a, 3 * W_SB // LANES, cum_col, cum_row, g[W_SB:W_SB + W_FOX])
        o_dl = _dil_attention(proj_b, cos_t, sin_t, g[W_SB + W_FOX:])
        x1, h2, h2_slabs, idx, gates, rank, counts = _output_projection(
            o_sb, o_fx, o_dl, w_out[l], x, gate1, scale2, shift2, ln1_g[l], ln1_b[l], w_router[l],
            router_bias[l])
        h2 = h2.reshape(n_tok, d)
        row_tok, block_e, n_used, dest_of, n_blocks = _dispatch_plan(
            idx.reshape(n_tok, 8)[:, :TOP_K], rank.reshape(n_tok, 8)[:, :TOP_K],
            counts[0, :N_EXPERTS].astype(I32), MOE_ROWS)
        y = _grouped_experts(h2_slabs, row_tok, block_e, n_used, n_blocks, l,
                             w_exp_gate, w_exp_up, w_exp_down)
        x = _combine(y, dest_of, gates.reshape(n_tok, 8), h2, x1.reshape(n_tok, d), gate2, ln2_g[l],
                     ln2_b[l], w_sh_gate[l], w_sh_up[l], w_sh_down[l], seq).reshape(bsz, seq, d)
    return x
```
